```python
import jax, jax.numpy as jnp
from jax import lax
import numpy as np

D_MODEL = 2048
BATCH = 8
SEQ = 8192
DEPTH = 1

SB_HEADS = 16
SB_HEAD_DIM = 128
SB_BLOCK = 128
GDN_HEADS = 16
GDN_K_DIM = 128
GDN_V_DIM = 128
GDN_CONV = 4
GDN_CHUNK = 64
D_FF = -(-8 * D_MODEL // (3 * 256)) * 256
EPS = 1e-6

D_SB = SB_HEADS * SB_HEAD_DIM
D_GDN_K = GDN_HEADS * GDN_K_DIM
D_GDN_V = GDN_HEADS * GDN_V_DIM
D_GDN_QKV = 2 * D_GDN_K + D_GDN_V
IN_SPLITS = (D_SB, D_SB, D_SB, D_GDN_QKV, D_GDN_V, GDN_HEADS, GDN_HEADS, D_MODEL, D_MODEL)
IN_OFFSETS = tuple(int(o) for o in np.cumsum(IN_SPLITS)[:-1])
D_IN = int(sum(IN_SPLITS))

kernel_name = 'hybrid_stickbreak_gdn_adaln_block'


def _rms(x, w):
    xf = x.astype(jnp.float32)
    return xf * lax.rsqrt(jnp.mean(xf * xf, axis=-1, keepdims=True) + EPS) * w.astype(jnp.float32)


def _l2norm(x):
    return x * lax.rsqrt(jnp.sum(x * x, axis=-1, keepdims=True) + EPS)


def _heads(t, n, d):
    b, s, _ = t.shape
    return t.reshape(b, s, n, d).transpose(0, 2, 1, 3)


def _merge_heads(t):
    b, n, s, d = t.shape
    return t.transpose(0, 2, 1, 3).reshape(b, s, n * d)


def _causal_depthwise_conv(x, w):
    return lax.conv_general_dilated(
        x, w[:, None, :].astype(x.dtype), window_strides=(1,), padding=[(GDN_CONV - 1, 0)],
        dimension_numbers=('NWC', 'WIO', 'NWC'), feature_group_count=x.shape[-1])


def _stick_breaking(q, k, v):
    s_len, d = q.shape[2], q.shape[3]
    scale = d ** -0.5
    outs = []
    for i in range(s_len // SB_BLOCK):
        q0 = i * SB_BLOCK
        kend = q0 + SB_BLOCK
        z = jnp.einsum('bhqd,bhkd->bhqk', q[:, :, q0:kend], k[:, :, :kend]) * scale
        causal = jnp.arange(kend)[None, :] < (q0 + jnp.arange(SB_BLOCK))[:, None]
        log_1m = jnp.where(causal, jax.nn.log_sigmoid(-z), 0.0)
        after = lax.cumsum(log_1m, axis=3, reverse=True) - log_1m
        w = jnp.where(causal, jnp.exp(jax.nn.log_sigmoid(z) + after), 0.0)
        outs.append(jnp.einsum('bhqk,bhkd->bhqd', w, v[:, :, :kend]))
    return jnp.concatenate(outs, axis=2)


def _gated_delta_rule(q, k, v, g, beta):
    b, h, s_len, dk = q.shape
    dv = v.shape[-1]
    n, c = s_len // GDN_CHUNK, GDN_CHUNK
    q = q.reshape(b, h, n, c, dk)
    k = k.reshape(b, h, n, c, dk)
    v = v.reshape(b, h, n, c, dv)
    g = g.reshape(b, h, n, c)
    beta = beta.reshape(b, h, n, c)
    gc = jnp.cumsum(g, axis=-1)
    tril = jnp.tril(jnp.ones((c, c), dtype=bool))
    stril = jnp.tril(jnp.ones((c, c), dtype=bool), k=-1)
    diff = gc[..., :, None] - gc[..., None, :]
    decay = jnp.where(tril, jnp.exp(jnp.where(tril, diff, 0.0)), 0.0)
    kk = jnp.einsum('bhnrd,bhnid->bhnri', k, k)
    m = jnp.eye(c, dtype=q.dtype) + jnp.where(stril, beta[..., :, None] * kk * decay, 0.0)
    solve = lambda rhs: lax.linalg.triangular_solve(m, rhs, left_side=True, lower=True, unit_diagonal=True)
    w_v = solve(beta[..., None] * v)
    w_k = solve((beta * jnp.exp(gc))[..., None] * k)
    attn = jnp.einsum('bhnrd,bhnid->bhnri', q, k) * decay
    q_g = q * jnp.exp(gc)[..., None]
    k_dec = k * jnp.exp(gc[..., -1:] - gc)[..., None]
    g_last = jnp.exp(gc[..., -1])

    def step(state, xs):
        w_v_c, w_k_c, q_g_c, attn_c, k_dec_c, g_last_c = xs
        u = w_v_c - jnp.einsum('bhcd,bhde->bhce', w_k_c, state)
        o = jnp.einsum('bhcd,bhde->bhce', q_g_c, state) + jnp.einsum('bhcj,bhje->bhce', attn_c, u)
        state = g_last_c[..., None, None] * state + jnp.einsum('bhcd,bhce->bhde', k_dec_c, u)
        return state, o

    mv = lambda t: jnp.moveaxis(t, 2, 0)
    state0 = jnp.zeros((b, h, dk, dv), dtype=q.dtype)
    _, o = lax.scan(step, state0, (mv(w_v), mv(w_k), mv(q_g), mv(attn), mv(k_dec), mv(g_last)))
    return jnp.moveaxis(o, 0, 2).reshape(b, h, s_len, dv)


def _fwd_setup_inputs(seed: int = 0) -> dict:
    key = jax.random.key(seed)
    ks = jax.random.split(key, 20)
    f32 = jnp.float32
    nrm = lambda k, shape, s: jax.random.normal(k, shape, f32) * s
    gain = lambda k, shape: 1.0 + 0.02 * jax.random.normal(k, shape, f32)
    dt = jnp.exp(jax.random.uniform(ks[10], (DEPTH, GDN_HEADS), f32, np.log(1e-3), np.log(1e-1)))
    return {
        'x': nrm(ks[0], (BATCH, SEQ, D_MODEL), 1.0),
        'c': nrm(ks[1], (BATCH, D_MODEL), 1.0),
        'w_mod': nrm(ks[2], (DEPTH, D_MODEL, 6 * D_MODEL), 0.5 * D_MODEL ** -0.5),
        'b_mod': nrm(ks[3], (DEPTH, 6 * D_MODEL), 0.02),
        'norm1_w': gain(ks[4], (DEPTH, D_MODEL)),
        'w_in': nrm(ks[5], (DEPTH, D_MODEL, D_IN), D_MODEL ** -0.5),
        'q_norm_w': gain(ks[6], (DEPTH, SB_HEAD_DIM)),
        'k_norm_w': gain(ks[7], (DEPTH, SB_HEAD_DIM)),
        'conv_w': nrm(ks[8], (DEPTH, GDN_CONV, D_GDN_QKV), GDN_CONV ** -0.5),
        'a_log': jnp.log(jax.random.uniform(ks[9], (DEPTH, GDN_HEADS), f32, 1.0, 16.0)),
        'dt_bias': dt + jnp.log(-jnp.expm1(-dt)),
        'o_norm_w': gain(ks[11], (DEPTH, GDN_V_DIM)),
        'p_a': nrm(ks[12], (DEPTH, D_SB, D_MODEL), D_SB ** -0.5),
        'p_b': nrm(ks[13], (DEPTH, D_GDN_V, D_MODEL), D_GDN_V ** -0.5),
        'w_out': nrm(ks[14], (DEPTH, D_MODEL, D_MODEL), D_MODEL ** -0.5),
        'norm2_w': gain(ks[15], (DEPTH, D_MODEL)),
        'w_gate': nrm(ks[16], (DEPTH, D_MODEL, D_FF), D_MODEL ** -0.5),
        'w_up': nrm(ks[17], (DEPTH, D_MODEL, D_FF), D_MODEL ** -0.5),
        'w_down': nrm(ks[18], (DEPTH, D_FF, D_MODEL), D_FF ** -0.5),
    }


def _fwd_reference(x, c, w_mod, b_mod, norm1_w, w_in, q_norm_w, k_norm_w, conv_w, a_log, dt_bias,
              o_norm_w, p_a, p_b, w_out, norm2_w, w_gate, w_up, w_down):
    f32 = jnp.float32
    bsz, s_len, _ = x.shape
    h = x.astype(f32)
    c_act = jax.nn.silu(c.astype(f32))
    for l in range(DEPTH):
        mod = c_act @ w_mod[l].astype(f32) + b_mod[l].astype(f32)
        shift1, scale1, gate1, shift2, scale2, gate2 = [t[:, None, :] for t in jnp.split(mod, 6, axis=-1)]

        u = _rms(h, norm1_w[l]) * (1.0 + scale1) + shift1
        proj = u @ w_in[l].astype(f32)
        qa, ka, va, qkv_b, z_b, b_b, a_b, gate_a, gate_b = jnp.split(proj, IN_OFFSETS, axis=-1)

        qa = _rms(_heads(qa, SB_HEADS, SB_HEAD_DIM), q_norm_w[l])
        ka = _rms(_heads(ka, SB_HEADS, SB_HEAD_DIM), k_norm_w[l])
        va = _heads(va, SB_HEADS, SB_HEAD_DIM)
        o_a = _merge_heads(_stick_breaking(qa, ka, va))

        qkv_b = jax.nn.silu(_causal_depthwise_conv(qkv_b, conv_w[l]))
        qb, kb, vb = jnp.split(qkv_b, (D_GDN_K, 2 * D_GDN_K), axis=-1)
        qb = _l2norm(_heads(qb, GDN_HEADS, GDN_K_DIM)) * (GDN_K_DIM ** -0.5)
        kb = _l2norm(_heads(kb, GDN_HEADS, GDN_K_DIM))
        vb = _heads(vb, GDN_HEADS, GDN_V_DIM)
        beta = jax.nn.sigmoid(b_b).transpose(0, 2, 1)
        g = (-jnp.exp(a_log[l].astype(f32)) * jax.nn.softplus(a_b + dt_bias[l].astype(f32))).transpose(0, 2, 1)
        o_b = _gated_delta_rule(qb, kb, vb, g, beta)
        o_b = _rms(o_b, o_norm_w[l]) * jax.nn.silu(_heads(z_b, GDN_HEADS, GDN_V_DIM))
        o_b = _merge_heads(o_b)

        merged = (jax.nn.sigmoid(gate_a) * (o_a @ p_a[l].astype(f32))
                  + jax.nn.sigmoid(gate_b) * (o_b @ p_b[l].astype(f32)))
        h = h + gate1 * (merged @ w_out[l].astype(f32))

        u = _rms(h, norm2_w[l]) * (1.0 + scale2) + shift2
        ff = jax.nn.silu(u @ w_gate[l].astype(f32)) * (u @ w_up[l].astype(f32))
        h = h + gate2 * (ff @ w_down[l].astype(f32))
    return h.astype(x.dtype)


import jax as _jax
import jax.numpy as _jnp

TWIN_FORMAT = 'train_step'
FWD_PARAMS = ['x', 'c', 'w_mod', 'b_mod', 'norm1_w', 'w_in', 'q_norm_w', 'k_norm_w', 'conv_w', 'a_log', 'dt_bias', 'o_norm_w', 'p_a', 'p_b', 'w_out', 'norm2_w', 'w_gate', 'w_up', 'w_down']
TWIN_WEIGHTS = ['w_mod', 'b_mod', 'norm1_w', 'w_in', 'q_norm_w', 'k_norm_w', 'conv_w', 'a_log', 'dt_bias', 'o_norm_w', 'p_a', 'p_b', 'w_out', 'norm2_w', 'w_gate', 'w_up', 'w_down']
TWIN_DIFF_INPUT = 'x'
TWIN_INPUTS = ['x', 'c', 'w_mod', 'b_mod', 'norm1_w', 'w_in', 'q_norm_w', 'k_norm_w', 'conv_w', 'a_log', 'dt_bias', 'o_norm_w', 'p_a', 'p_b', 'w_out', 'norm2_w', 'w_gate', 'w_up', 'w_down', 'loss_target', 'm_w_mod', 'm_b_mod', 'm_norm1_w', 'm_w_in', 'm_q_norm_w', 'm_k_norm_w', 'm_conv_w', 'm_a_log', 'm_dt_bias', 'm_o_norm_w', 'm_p_a', 'm_p_b', 'm_w_out', 'm_norm2_w', 'm_w_gate', 'm_w_up', 'm_w_down', 'v_w_mod', 'v_b_mod', 'v_norm1_w', 'v_w_in', 'v_q_norm_w', 'v_k_norm_w', 'v_conv_w', 'v_a_log', 'v_dt_bias', 'v_o_norm_w', 'v_p_a', 'v_p_b', 'v_w_out', 'v_norm2_w', 'v_w_gate', 'v_w_up', 'v_w_down']
TWIN_OUTPUTS = ['loss', 'grad_x', 'grad_w_mod', 'grad_b_mod', 'grad_norm1_w', 'grad_w_in', 'grad_q_norm_w', 'grad_k_norm_w', 'grad_conv_w', 'grad_a_log', 'grad_dt_bias', 'grad_o_norm_w', 'grad_p_a', 'grad_p_b', 'grad_w_out', 'grad_norm2_w', 'grad_w_gate', 'grad_w_up', 'grad_w_down', 'delta_w_mod', 'delta_b_mod', 'delta_norm1_w', 'delta_w_in', 'delta_q_norm_w', 'delta_k_norm_w', 'delta_conv_w', 'delta_a_log', 'delta_dt_bias', 'delta_o_norm_w', 'delta_p_a', 'delta_p_b', 'delta_w_out', 'delta_norm2_w', 'delta_w_gate', 'delta_w_up', 'delta_w_down', 'new_m_w_mod', 'new_m_b_mod', 'new_m_norm1_w', 'new_m_w_in', 'new_m_q_norm_w', 'new_m_k_norm_w', 'new_m_conv_w', 'new_m_a_log', 'new_m_dt_bias', 'new_m_o_norm_w', 'new_m_p_a', 'new_m_p_b', 'new_m_w_out', 'new_m_norm2_w', 'new_m_w_gate', 'new_m_w_up', 'new_m_w_down', 'new_v_w_mod', 'new_v_b_mod', 'new_v_norm1_w', 'new_v_w_in', 'new_v_q_norm_w', 'new_v_k_norm_w', 'new_v_conv_w', 'new_v_a_log', 'new_v_dt_bias', 'new_v_o_norm_w', 'new_v_p_a', 'new_v_p_b', 'new_v_w_out', 'new_v_norm2_w', 'new_v_w_gate', 'new_v_w_up', 'new_v_w_down']
TWIN_LEAF_KINDS = {'loss': 'loss', 'grad_x': 'grad_x', 'grad_w_mod': 'grad_w', 'grad_b_mod': 'grad_w', 'grad_norm1_w': 'grad_w', 'grad_w_in': 'grad_w', 'grad_q_norm_w': 'grad_w', 'grad_k_norm_w': 'grad_w', 'grad_conv_w': 'grad_w', 'grad_a_log': 'grad_w', 'grad_dt_bias': 'grad_w', 'grad_o_norm_w': 'grad_w', 'grad_p_a': 'grad_w', 'grad_p_b': 'grad_w', 'grad_w_out': 'grad_w', 'grad_norm2_w': 'grad_w', 'grad_w_gate': 'grad_w', 'grad_w_up': 'grad_w', 'grad_w_down': 'grad_w', 'delta_w_mod': 'delta_w', 'delta_b_mod': 'delta_w', 'delta_norm1_w': 'delta_w', 'delta_w_in': 'delta_w', 'delta_q_norm_w': 'delta_w', 'delta_k_norm_w': 'delta_w', 'delta_conv_w': 'delta_w', 'delta_a_log': 'delta_w', 'delta_dt_bias': 'delta_w', 'delta_o_norm_w': 'delta_w', 'delta_p_a': 'delta_w', 'delta_p_b': 'delta_w', 'delta_w_out': 'delta_w', 'delta_norm2_w': 'delta_w', 'delta_w_gate': 'delta_w', 'delta_w_up': 'delta_w', 'delta_w_down': 'delta_w', 'new_m_w_mod': 'new_m', 'new_m_b_mod': 'new_m', 'new_m_norm1_w': 'new_m', 'new_m_w_in': 'new_m', 'new_m_q_norm_w': 'new_m', 'new_m_k_norm_w': 'new_m', 'new_m_conv_w': 'new_m', 'new_m_a_log': 'new_m', 'new_m_dt_bias': 'new_m', 'new_m_o_norm_w': 'new_m', 'new_m_p_a': 'new_m', 'new_m_p_b': 'new_m', 'new_m_w_out': 'new_m', 'new_m_norm2_w': 'new_m', 'new_m_w_gate': 'new_m', 'new_m_w_up': 'new_m', 'new_m_w_down': 'new_m', 'new_v_w_mod': 'new_v', 'new_v_b_mod': 'new_v', 'new_v_norm1_w': 'new_v', 'new_v_w_in': 'new_v', 'new_v_q_norm_w': 'new_v', 'new_v_k_norm_w': 'new_v', 'new_v_conv_w': 'new_v', 'new_v_a_log': 'new_v', 'new_v_dt_bias': 'new_v', 'new_v_o_norm_w': 'new_v', 'new_v_p_a': 'new_v', 'new_v_p_b': 'new_v', 'new_v_w_out': 'new_v', 'new_v_norm2_w': 'new_v', 'new_v_w_gate': 'new_v', 'new_v_w_up': 'new_v', 'new_v_w_down': 'new_v'}


def _forward(args):
    return _fwd_reference(*[args[k] for k in FWD_PARAMS])


def _output_shape():
    def fwd():
        inp = _fwd_setup_inputs(0)
        return _fwd_reference(*[inp[k] for k in FWD_PARAMS])
    out = _jax.eval_shape(fwd)
    return out.shape, out.dtype

N_MICROBATCH = 1
ADAM_LR = 0.001
ADAM_B1 = 0.9
ADAM_B2 = 0.999
ADAM_EPS = 1e-08
ADAM_WD = 0.01
ADAM_STEP = 10
PER_EXAMPLE_BATCH_AXIS = {'x': 0, 'c': 0, 'loss_target': 0}
SHARED_INPUTS = []
_WEIGHT_DTYPES = {'w_mod': _jnp.float32, 'b_mod': _jnp.float32, 'norm1_w': _jnp.float32, 'w_in': _jnp.float32, 'q_norm_w': _jnp.float32, 'k_norm_w': _jnp.float32, 'conv_w': _jnp.float32, 'a_log': _jnp.float32, 'dt_bias': _jnp.float32, 'o_norm_w': _jnp.float32, 'p_a': _jnp.float32, 'p_b': _jnp.float32, 'w_out': _jnp.float32, 'norm2_w': _jnp.float32, 'w_gate': _jnp.float32, 'w_up': _jnp.float32, 'w_down': _jnp.float32}
MOMENT_SCALE = {'w_mod': 6.057519e-01, 'b_mod': 1.653108e+00, 'norm1_w': 9.140474e-01, 'w_in': 2.870101e-02, 'q_norm_w': 8.809259e-01, 'k_norm_w': 8.852029e-01, 'conv_w': 3.086432e-02, 'a_log': 9.419952e-01, 'dt_bias': 9.037174e-01, 'o_norm_w': 5.963577e+00, 'p_a': 4.724338e-02, 'p_b': 3.716435e-02, 'w_out': 5.372441e-02, 'norm2_w': 3.134177e+00, 'w_gate': 4.744467e-02, 'w_up': 3.040203e-02, 'w_down': 4.691809e-02}


def _to_microbatches(a, axis):
    t = _jnp.moveaxis(a, axis, 0)
    t = t.reshape((N_MICROBATCH, t.shape[0] // N_MICROBATCH) + t.shape[1:])
    return _jnp.moveaxis(t, 1, axis + 1)


def setup_inputs(seed: int = 0) -> dict:
    inp = _fwd_setup_inputs(seed)
    key = _jax.random.fold_in(_jax.random.key(seed), 7919)
    shape, _ = _output_shape()
    out = dict(inp)
    out["loss_target"] = _jax.random.normal(_jax.random.fold_in(key, 0), shape, _jnp.float32)
    for i, name in enumerate(TWIN_WEIGHTS):
        w = inp[name].astype(_jnp.float32)
        if MOMENT_SCALE is None:
            s = _jnp.sqrt(_jnp.mean(_jnp.square(w)) + 1e-30)
        else:
            s = MOMENT_SCALE[name]
        km, kv = _jax.random.split(_jax.random.fold_in(key, i + 1))
        out[name] = w
        out["m_" + name] = s * _jax.random.normal(km, w.shape, _jnp.float32)
        out["v_" + name] = (s * s) * _jax.random.uniform(kv, w.shape, _jnp.float32, 0.5, 1.5)
    if N_MICROBATCH > 1:
        for name, axis in PER_EXAMPLE_BATCH_AXIS.items():
            out[name] = _to_microbatches(out[name], axis)
    return {'x': out['x'], 'c': out['c'], 'w_mod': out['w_mod'], 'b_mod': out['b_mod'], 'norm1_w': out['norm1_w'], 'w_in': out['w_in'], 'q_norm_w': out['q_norm_w'], 'k_norm_w': out['k_norm_w'], 'conv_w': out['conv_w'], 'a_log': out['a_log'], 'dt_bias': out['dt_bias'], 'o_norm_w': out['o_norm_w'], 'p_a': out['p_a'], 'p_b': out['p_b'], 'w_out': out['w_out'], 'norm2_w': out['norm2_w'], 'w_gate': out['w_gate'], 'w_up': out['w_up'], 'w_down': out['w_down'], 'loss_target': out['loss_target'], 'm_w_mod': out['m_w_mod'], 'm_b_mod': out['m_b_mod'], 'm_norm1_w': out['m_norm1_w'], 'm_w_in': out['m_w_in'], 'm_q_norm_w': out['m_q_norm_w'], 'm_k_norm_w': out['m_k_norm_w'], 'm_conv_w': out['m_conv_w'], 'm_a_log': out['m_a_log'], 'm_dt_bias': out['m_dt_bias'], 'm_o_norm_w': out['m_o_norm_w'], 'm_p_a': out['m_p_a'], 'm_p_b': out['m_p_b'], 'm_w_out': out['m_w_out'], 'm_norm2_w': out['m_norm2_w'], 'm_w_gate': out['m_w_gate'], 'm_w_up': out['m_w_up'], 'm_w_down': out['m_w_down'], 'v_w_mod': out['v_w_mod'], 'v_b_mod': out['v_b_mod'], 'v_norm1_w': out['v_norm1_w'], 'v_w_in': out['v_w_in'], 'v_q_norm_w': out['v_q_norm_w'], 'v_k_norm_w': out['v_k_norm_w'], 'v_conv_w': out['v_conv_w'], 'v_a_log': out['v_a_log'], 'v_dt_bias': out['v_dt_bias'], 'v_o_norm_w': out['v_o_norm_w'], 'v_p_a': out['v_p_a'], 'v_p_b': out['v_p_b'], 'v_w_out': out['v_w_out'], 'v_norm2_w': out['v_norm2_w'], 'v_w_gate': out['v_w_gate'], 'v_w_up': out['v_w_up'], 'v_w_down': out['v_w_down']}


def _loss(weights, diff, rest, loss_target):
    with _jax.named_scope("forward"):
        args = {**rest, TWIN_DIFF_INPUT: diff, **{k: w.astype(_WEIGHT_DTYPES[k]) for k, w in weights.items()}}
        y = _forward(args)
    with _jax.named_scope("loss_head"):
        err = _jnp.square(y.astype(_jnp.float32) - loss_target)
        return 0.5 * _jnp.sum(_jnp.mean(err, axis=-1)) if err.ndim else 0.5 * err


def _adamw(w, g, m, v):
    m = ADAM_B1 * m + (1.0 - ADAM_B1) * g
    v = ADAM_B2 * v + (1.0 - ADAM_B2) * _jnp.square(g)
    m_hat = m / (1.0 - ADAM_B1 ** ADAM_STEP)
    v_hat = v / (1.0 - ADAM_B2 ** ADAM_STEP)
    delta = -ADAM_LR * (m_hat / (_jnp.sqrt(v_hat) + ADAM_EPS) + ADAM_WD * w)
    return delta, m, v


def reference(x, c, w_mod, b_mod, norm1_w, w_in, q_norm_w, k_norm_w, conv_w, a_log, dt_bias, o_norm_w, p_a, p_b, w_out, norm2_w, w_gate, w_up, w_down, loss_target, m_w_mod, m_b_mod, m_norm1_w, m_w_in, m_q_norm_w, m_k_norm_w, m_conv_w, m_a_log, m_dt_bias, m_o_norm_w, m_p_a, m_p_b, m_w_out, m_norm2_w, m_w_gate, m_w_up, m_w_down, v_w_mod, v_b_mod, v_norm1_w, v_w_in, v_q_norm_w, v_k_norm_w, v_conv_w, v_a_log, v_dt_bias, v_o_norm_w, v_p_a, v_p_b, v_w_out, v_norm2_w, v_w_gate, v_w_up, v_w_down):
    given = dict(x=x, c=c, w_mod=w_mod, b_mod=b_mod, norm1_w=norm1_w, w_in=w_in, q_norm_w=q_norm_w, k_norm_w=k_norm_w, conv_w=conv_w, a_log=a_log, dt_bias=dt_bias, o_norm_w=o_norm_w, p_a=p_a, p_b=p_b, w_out=w_out, norm2_w=norm2_w, w_gate=w_gate, w_up=w_up, w_down=w_down, loss_target=loss_target, m_w_mod=m_w_mod, m_b_mod=m_b_mod, m_norm1_w=m_norm1_w, m_w_in=m_w_in, m_q_norm_w=m_q_norm_w, m_k_norm_w=m_k_norm_w, m_conv_w=m_conv_w, m_a_log=m_a_log, m_dt_bias=m_dt_bias, m_o_norm_w=m_o_norm_w, m_p_a=m_p_a, m_p_b=m_p_b, m_w_out=m_w_out, m_norm2_w=m_norm2_w, m_w_gate=m_w_gate, m_w_up=m_w_up, m_w_down=m_w_down, v_w_mod=v_w_mod, v_b_mod=v_b_mod, v_norm1_w=v_norm1_w, v_w_in=v_w_in, v_q_norm_w=v_q_norm_w, v_k_norm_w=v_k_norm_w, v_conv_w=v_conv_w, v_a_log=v_a_log, v_dt_bias=v_dt_bias, v_o_norm_w=v_o_norm_w, v_p_a=v_p_a, v_p_b=v_p_b, v_w_out=v_w_out, v_norm2_w=v_norm2_w, v_w_gate=v_w_gate, v_w_up=v_w_up, v_w_down=v_w_down)
    weights = {n: given[n] for n in TWIN_WEIGHTS}
    shared = {n: given[n] for n in SHARED_INPUTS}
    per_example = {n: given[n] for n in ['x', 'c']}
    grad_fn = _jax.value_and_grad(_loss, argnums=(0, 1))

    def one_microbatch(ex, loss_target):
        ex = dict(ex)
        diff = ex.pop(TWIN_DIFF_INPUT)
        return grad_fn(weights, diff, {**shared, **ex}, loss_target)

    if N_MICROBATCH == 1:
        loss, (grad_w, grad_x) = one_microbatch(per_example, given["loss_target"])
    else:
        def body(carry, xs):
            loss_sum, grad_sum = carry
            l_k, (gw_k, gx_k) = one_microbatch(xs[0], xs[1])
            with _jax.named_scope("update"):
                return (loss_sum + l_k, _jax.tree.map(_jnp.add, grad_sum, gw_k)), gx_k

        init = (_jnp.zeros((), _jnp.float32), _jax.tree.map(_jnp.zeros_like, weights))
        (loss, grad_w), grad_x = _jax.lax.scan(body, init, (per_example, given["loss_target"]))
    with _jax.named_scope("update"):
        delta_w, new_m, new_v = {}, {}, {}
        for n in TWIN_WEIGHTS:
            delta_w[n], new_m[n], new_v[n] = _adamw(weights[n], grad_w[n], given["m_" + n], given["v_" + n])
    return (loss, grad_x, *[grad_w[n] for n in TWIN_WEIGHTS], *[delta_w[n] for n in TWIN_WEIGHTS],
            *[new_m[n] for n in TWIN_WEIGHTS], *[new_v[n] for n in TWIN_WEIGHTS])
```

```python
import jax
import jax.numpy as jnp
from jax import lax
from jax.experimental import pallas as pl
from jax.experimental.pallas import tpu as pltpu

F32, BF16 = jnp.float32, jnp.bfloat16
EPS = 1e-6
HEAD_DIM = 128
GDN_CHUNK = 64
GDN_GROUP = 256
CONV_TAPS = 4
N_DEV = 8
MESH_AXES = ("x", "y", "c")
ADAM_LR, ADAM_B1, ADAM_B2, ADAM_EPS, ADAM_WD, ADAM_STEP = 0.001, 0.9, 0.999, 1e-08, 0.01, 10
VMEM_LIMIT_BYTES = 56 * 1024 * 1024
LANE = 128
MESH_ID = pl.DeviceIdType.MESH


def _call(body, **kw):
    return pl.pallas_call(body, **kw)


def _params(*sem):
    return pltpu.CompilerParams(dimension_semantics=sem or None, vmem_limit_bytes=VMEM_LIMIT_BYTES)


def _tile(n, target):
    t = (min(n, target) // LANE) * LANE
    while t >= LANE:
        if n % t == 0:
            return t
        t -= LANE
    return n


def _silu(x):
    return x * jax.nn.sigmoid(x)


def _my_id():
    return 4 * lax.axis_index("x") + 2 * lax.axis_index("y") + lax.axis_index("c")


def _exchange(name, src, scatter):
    blk = src.shape[1:] if scatter else src.shape

    def body(src_ref, out_ref, send_sems, recv_sems, local_sem):
        x, y, c = lax.axis_index("x"), lax.axis_index("y"), lax.axis_index("c")
        me = 4 * x + 2 * y + c

        def peer(k):
            kx, ky, kc = (k >> 2) & 1, (k >> 1) & 1, k & 1
            px, py, pc = x ^ kx, y ^ ky, c ^ kc
            return (px, py, pc), 4 * px + 2 * py + pc

        def copy(k):
            dev, pid = peer(k)
            return pltpu.make_async_remote_copy(
                src_ref=src_ref.at[pid] if scatter else src_ref,
                dst_ref=out_ref.at[me],
                send_sem=send_sems.at[k - 1],
                recv_sem=recv_sems.at[k - 1],
                device_id=dev,
                device_id_type=MESH_ID,
            )

        def arrival(k):
            dev, pid = peer(k)
            return pltpu.make_async_remote_copy(
                src_ref=src_ref.at[pid] if scatter else src_ref,
                dst_ref=out_ref.at[pid],
                send_sem=send_sems.at[k - 1],
                recv_sem=recv_sems.at[k - 1],
                device_id=dev,
                device_id_type=MESH_ID,
            )

        mine = pltpu.make_async_copy(src_ref.at[me] if scatter else src_ref, out_ref.at[me], local_sem)
        mine.start()
        for k in range(1, N_DEV):
            copy(k).start()
        for k in range(1, N_DEV):
            arrival(k).wait_recv()
        for k in range(1, N_DEV):
            copy(k).wait_send()
        mine.wait()

    return _call(
        body,
        name=name,
        out_shape=jax.ShapeDtypeStruct((N_DEV,) + tuple(blk), src.dtype),
        in_specs=[pl.BlockSpec(memory_space=pl.ANY)],
        out_specs=pl.BlockSpec(memory_space=pl.ANY),
        scratch_shapes=[
            pltpu.SemaphoreType.DMA((N_DEV - 1,)),
            pltpu.SemaphoreType.DMA((N_DEV - 1,)),
            pltpu.SemaphoreType.DMA,
        ],
    )(src)


def _mm(name, a, b, *, ta=False, tb=False, out_dtype=F32, a_fn=None, tm=512, tn=512, tk=512):
    m, kdim = (a.shape[1], a.shape[0]) if ta else a.shape
    n = b.shape[0] if tb else b.shape[1]
    assert kdim == (b.shape[1] if tb else b.shape[0]), (a.shape, b.shape, ta, tb)
    tm, tn, tk = _tile(m, tm), _tile(n, tn), _tile(kdim, tk)
    nk = kdim // tk
    a_spec = pl.BlockSpec((tk, tm), lambda i, j, k: (k, i)) if ta else pl.BlockSpec((tm, tk), lambda i, j, k: (i, k))
    b_spec = pl.BlockSpec((tn, tk), lambda i, j, k: (j, k)) if tb else pl.BlockSpec((tk, tn), lambda i, j, k: (k, j))
    dims = (((0 if ta else 1,), (1 if tb else 0,)), ((), ()))

    def body(a_ref, b_ref, o_ref, acc_ref):
        k = pl.program_id(2)
        av = a_ref[...]
        if a_fn is not None:
            av = a_fn(av.astype(F32))
        p = lax.dot_general(av.astype(BF16), b_ref[...].astype(BF16), dims, preferred_element_type=F32)
        if nk == 1:
            o_ref[...] = p.astype(out_dtype)
        else:
            @pl.when(k == 0)
            def _():
                acc_ref[...] = p

            @pl.when(k > 0)
            def _():
                acc_ref[...] += p

            @pl.when(k == nk - 1)
            def _():
                o_ref[...] = acc_ref[...].astype(out_dtype)

    return _call(
        body,
        name=name,
        grid=(m // tm, n // tn, nk),
        in_specs=[a_spec, b_spec],
        out_specs=pl.BlockSpec((tm, tn), lambda i, j, k: (i, j)),
        out_shape=jax.ShapeDtypeStruct((m, n), out_dtype),
        scratch_shapes=[pltpu.VMEM((tm, tn) if nk > 1 else (8, LANE), F32)],
        compiler_params=_params("parallel", "parallel", "arbitrary"),
    )(a, b)


def _row(arr, w=None, off=0, stride=0):
    return (arr, "row", arr.shape[1] if w is None else w, off, stride)


def _par(arr, w=None, off=0, stride=0):
    return (arr, "par", arr.shape[1] if w is None else w, off, stride)


def _seg(name, fn, ins, outs, rows, tm, nh=1):
    nrow = rows // tm

    def spec(kind, w, off, stride):
        if kind == "row":
            return pl.BlockSpec((tm, w), lambda h, i: (i, off + stride * h))
        return pl.BlockSpec((1, w), lambda h, i: (0, off + stride * h))

    in_specs = [spec(kind, w, off, stride) for (_, kind, w, off, stride) in ins]
    out_specs = [spec("row" if kind == "row" else "par", w, off, stride) for (_, _, kind, w, off, stride) in outs]
    out_shape = [
        jax.ShapeDtypeStruct((rows if kind == "row" else 1, ncols), dt) for (ncols, dt, kind, _, _, _) in outs
    ]
    n_in = len(ins)

    def body(*refs):
        h, i = pl.program_id(0), pl.program_id(1)
        vals = fn(*[r[...] for r in refs[:n_in]])
        for (_, dt, kind, _, _, stride), ref, val in zip(outs, refs[n_in:], vals):
            if kind == "row":
                ref[...] = val.astype(dt)
            else:
                first = (i == 0) if stride != 0 else jnp.logical_and(i == 0, h == 0)

                @pl.when(first)
                def _(ref=ref, val=val):
                    ref[...] = val.astype(F32)

                @pl.when(jnp.logical_not(first))
                def _(ref=ref, val=val):
                    ref[...] += val.astype(F32)

    res = _call(
        body,
        name=name,
        grid=(nh, nrow),
        in_specs=in_specs,
        out_specs=out_specs,
        out_shape=out_shape,
        compiler_params=_params("arbitrary", "arbitrary"),
    )(*[t[0] for t in ins])
    return res


def _rmsn(x, w):
    return x * lax.rsqrt(jnp.mean(x * x, axis=-1, keepdims=True) + EPS) * w


def _f_mod(x, nw, sc, sh):
    return _rmsn(x, nw) * (1.0 + sc) + sh


def _f_res(x, t, g, nw, sc, sh):
    h = x + g * t
    return h, _f_mod(h, nw, sc, sh)


def _f_ff(gg, uu):
    return _silu(gg) * uu


def _f_merge(ya, yb, ga, gb):
    return jax.nn.sigmoid(ga) * ya + jax.nn.sigmoid(gb) * yb


def _f_qk(q, k, qw, kw):
    return _rmsn(q, qw), _rmsn(k, kw)


def _f_post(ob, z, ow):
    return _rmsn(ob, ow) * _silu(z)


def _f32(*xs):
    return [x.astype(F32) for x in xs]


def _log_sigmoid(z):
    return jnp.minimum(z, 0.0) - jnp.log(1.0 + jnp.exp(-jnp.abs(z)))


def _dot2(x, tri):
    hi = x.astype(BF16)
    lo = (x - hi.astype(F32)).astype(BF16)
    return jnp.dot(hi, tri, preferred_element_type=F32) + jnp.dot(lo, tri, preferred_element_type=F32)


def _tri(strict):
    j = lax.broadcasted_iota(jnp.int32, (LANE, LANE), 0)
    s = lax.broadcasted_iota(jnp.int32, (LANE, LANE), 1)
    return ((j > s) if strict else (j >= s)).astype(BF16)


_NT = (((1,), (1,)), ((), ()))
_TN = (((0,), (0,)), ((), ()))


def _sb_fwd(qn, kn, vb, heads, bq):
    s_len = qn.shape[0]
    scale = HEAD_DIM ** -0.5
    sub = bq // LANE

    def body(q_ref, k_ref, v_ref, o_ref):
        i = pl.program_id(1)
        q = q_ref[...]
        row = i * bq + lax.broadcasted_iota(jnp.int32, (bq, LANE), 0)
        lane = lax.broadcasted_iota(jnp.int32, (bq, LANE), 1)
        tri = _tri(True)
        nsub = (i + 1) * sub

        def step(t, carry):
            cl, acc = carry
            j0 = pl.multiple_of((nsub - 1 - t) * LANE, LANE)
            k = k_ref[pl.ds(j0, LANE), :]
            v = v_ref[pl.ds(j0, LANE), :]
            z = lax.dot_general(q, k, _NT, preferred_element_type=F32) * scale
            causal = (j0 + lane) < row
            ls = _log_sigmoid(z)
            lm = jnp.where(causal, ls - z, 0.0)
            after = _dot2(lm, tri) + cl
            w = jnp.where(causal, jnp.exp(ls + after), 0.0)
            acc = acc + jnp.dot(w.astype(BF16), v, preferred_element_type=F32)
            return cl + jnp.sum(lm, axis=1, keepdims=True), acc

        _, acc = lax.fori_loop(0, nsub, step, (jnp.zeros((bq, 1), F32), jnp.zeros((bq, LANE), F32)))
        o_ref[...] = acc

    return _call(
        body,
        name="sb_fwd",
        grid=(heads, s_len // bq),
        in_specs=[
            pl.BlockSpec((bq, LANE), lambda h, i: (i, h)),
            pl.BlockSpec((s_len, LANE), lambda h, i: (0, h)),
            pl.BlockSpec((s_len, LANE), lambda h, i: (0, h)),
        ],
        out_specs=pl.BlockSpec((bq, LANE), lambda h, i: (i, h)),
        out_shape=jax.ShapeDtypeStruct(qn.shape, F32),
        compiler_params=_params("parallel", "arbitrary"),
    )(qn, kn, vb)


def _sb_bwd(qn, kn, vb, do, heads, bq):
    s_len = qn.shape[0]
    scale = HEAD_DIM ** -0.5
    sub = bq // LANE

    def body(q_ref, k_ref, v_ref, do_ref, dq_ref, dk_ref, dv_ref):
        i = pl.program_id(1)

        @pl.when(i == 0)
        def _():
            dk_ref[...] = jnp.zeros_like(dk_ref)
            dv_ref[...] = jnp.zeros_like(dv_ref)

        q = q_ref[...]
        dob = do_ref[...].astype(BF16)
        row = i * bq + lax.broadcasted_iota(jnp.int32, (bq, LANE), 0)
        lane = lax.broadcasted_iota(jnp.int32, (bq, LANE), 1)
        tri, tri_inc = _tri(True), _tri(False)
        nsub = (i + 1) * sub

        def tile_terms(t, cl):
            j0 = pl.multiple_of((nsub - 1 - t) * LANE, LANE)
            k = k_ref[pl.ds(j0, LANE), :]
            v = v_ref[pl.ds(j0, LANE), :]
            z = lax.dot_general(q, k, _NT, preferred_element_type=F32) * scale
            causal = (j0 + lane) < row
            ls = _log_sigmoid(z)
            lm = jnp.where(causal, ls - z, 0.0)
            a = jnp.where(causal, jnp.exp(ls + (_dot2(lm, tri) + cl)), 0.0)
            e = a * lax.dot_general(dob, v, _NT, preferred_element_type=F32)
            return j0, k, z, causal, ls, a, e, cl + jnp.sum(lm, axis=1, keepdims=True)

        def sweep_total(t, carry):
            cl, ce = carry
            j0, _, _, _, _, a, e, cl = tile_terms(t, cl)
            dv_ref[pl.ds(j0, LANE), :] += lax.dot_general(a.astype(BF16), dob, _TN, preferred_element_type=F32)
            return cl, ce + jnp.sum(e, axis=1, keepdims=True)

        zero = jnp.zeros((bq, 1), F32)
        _, etot = lax.fori_loop(0, nsub, sweep_total, (zero, zero))

        def sweep_grad(t, carry):
            cl, ce, dq = carry
            j0, k, z, causal, ls, _, e, cl = tile_terms(t, cl)
            before = etot - (_dot2(e, tri_inc) + ce)
            dz = jnp.where(causal, e * jnp.exp(ls - z) - jnp.exp(ls) * before, 0.0) * scale
            dzb = dz.astype(BF16)
            dq = dq + jnp.dot(dzb, k, preferred_element_type=F32)
            dk_ref[pl.ds(j0, LANE), :] += lax.dot_general(dzb, q, _TN, preferred_element_type=F32)
            return cl, ce + jnp.sum(e, axis=1, keepdims=True), dq

        _, _, dq = lax.fori_loop(0, nsub, sweep_grad, (zero, zero, jnp.zeros((bq, LANE), F32)))
        dq_ref[...] = dq

    tile = pl.BlockSpec((bq, LANE), lambda h, i: (i, h))
    full = pl.BlockSpec((s_len, LANE), lambda h, i: (0, h))
    shp = jax.ShapeDtypeStruct(qn.shape, F32)
    return _call(
        body,
        name="sb_bwd",
        grid=(heads, s_len // bq),
        in_specs=[tile, full, full, tile],
        out_specs=[tile, full, full],
        out_shape=[shp, shp, shp],
        compiler_params=_params("parallel", "arbitrary"),
    )(qn, kn, vb, do)


def _shift_down(cur, halo, k):
    if k == 0:
        return cur
    r = pltpu.roll(cur, k, 0)
    p = pltpu.roll(halo, k, 0)
    top = jnp.where(lax.broadcasted_iota(jnp.int32, halo.shape, 0) < k, p, r[:8])
    return jnp.concatenate([top, r[8:]], axis=0)


def _shift_up(cur, halo, k):
    if k == 0:
        return cur
    n = cur.shape[0]
    r = pltpu.roll(cur, n - k, 0)
    p = pltpu.roll(halo, 8 - k, 0)
    bot = jnp.where(lax.broadcasted_iota(jnp.int32, halo.shape, 0) >= 8 - k, p, r[n - 8:])
    return jnp.concatenate([r[: n - 8], bot], axis=0)


def _f_qkv_act(pre, jb, heads):
    act = _silu(pre)
    nrm = act * lax.rsqrt(jnp.sum(act * act, axis=-1, keepdims=True) + EPS)
    nrm = nrm * jnp.where(jb < heads, HEAD_DIM ** -0.5, 1.0)
    return jnp.where(jb < 2 * heads, nrm, act)


def _taps(w_ref):
    return [w_ref[j:j + 1, :] for j in range(CONV_TAPS)]


def _conv_pre(x_ref, w, r0, tr):
    cur = x_ref[pl.ds(r0, tr), :]
    halo = x_ref[pl.ds(jnp.maximum(r0 - 8, 0), 8), :]
    halo = jnp.where(r0 > 0, halo, 0.0)
    shifted = [_shift_down(cur, halo, CONV_TAPS - 1 - j) for j in range(CONV_TAPS)]
    pre = sum(w[j] * shifted[j] for j in range(CONV_TAPS))
    return pre, shifted


def _bprep_fwd(proj, col0, conv_w8, heads, tr):
    s_len = proj.shape[0]
    ncol = 3 * heads
    cb0 = col0 // LANE

    def body(x_ref, w_ref, o_ref):
        jb = pl.program_id(0)
        w = _taps(w_ref)

        def step(t, _):
            r0 = pl.multiple_of(t * tr, tr)
            pre, _unused = _conv_pre(x_ref, w, r0, tr)
            o_ref[pl.ds(r0, tr), :] = _f_qkv_act(pre, jb, heads)
            return 0

        lax.fori_loop(0, s_len // tr, step, 0)

    return _call(
        body,
        name="bprep_fwd",
        grid=(ncol,),
        in_specs=[
            pl.BlockSpec((s_len, LANE), lambda j: (0, cb0 + j)),
            pl.BlockSpec((8, LANE), lambda j: (0, j)),
        ],
        out_specs=pl.BlockSpec((s_len, LANE), lambda j: (0, j)),
        out_shape=jax.ShapeDtypeStruct((s_len, ncol * LANE), F32),
        compiler_params=_params("parallel"),
    )(proj, conv_w8)


def _bprep_bwd(proj, col0, conv_w8, dqkv, heads, tr):
    s_len = proj.shape[0]
    ncol = 3 * heads
    cb0 = col0 // LANE
    nt = s_len // tr

    def body(x_ref, w_ref, d_ref, dx_ref, dw_ref, dpre_ref):
        jb = pl.program_id(0)
        w = _taps(w_ref)

        def pass1(t, dws):
            r0 = pl.multiple_of(t * tr, tr)
            pre, shifted = _conv_pre(x_ref, w, r0, tr)
            _, vjp = jax.vjp(lambda p: _f_qkv_act(p, jb, heads), pre)
            (dpre,) = vjp(d_ref[pl.ds(r0, tr), :])
            dpre_ref[pl.ds(r0, tr), :] = dpre
            return tuple(dws[j] + jnp.sum(dpre * shifted[j], axis=0, keepdims=True) for j in range(CONV_TAPS))

        dws = lax.fori_loop(0, nt, pass1, tuple(jnp.zeros((1, LANE), F32) for _ in range(CONV_TAPS)))
        for j in range(CONV_TAPS):
            dw_ref[j:j + 1, :] = dws[j]
        dw_ref[CONV_TAPS:, :] = jnp.zeros((8 - CONV_TAPS, LANE), F32)

        def pass2(t, _):
            r0 = pl.multiple_of(t * tr, tr)
            cur = dpre_ref[pl.ds(r0, tr), :]
            halo = dpre_ref[pl.ds(jnp.minimum(r0 + tr, s_len - 8), 8), :]
            halo = jnp.where(r0 + tr < s_len, halo, 0.0)
            dx = sum(w[j] * _shift_up(cur, halo, CONV_TAPS - 1 - j) for j in range(CONV_TAPS))
            dx_ref[pl.ds(r0, tr), :] = dx.astype(BF16)
            return 0

        lax.fori_loop(0, nt, pass2, 0)

    col = pl.BlockSpec((s_len, LANE), lambda j: (0, j))
    w_spec = pl.BlockSpec((8, LANE), lambda j: (0, j))
    return _call(
        body,
        name="bprep_bwd",
        grid=(ncol,),
        in_specs=[pl.BlockSpec((s_len, LANE), lambda j: (0, cb0 + j)), w_spec, col],
        out_specs=[col, w_spec],
        out_shape=[jax.ShapeDtypeStruct((s_len, ncol * LANE), BF16), jax.ShapeDtypeStruct((8, ncol * LANE), F32)],
        scratch_shapes=[pltpu.VMEM((s_len, LANE), F32)],
        compiler_params=_params("parallel"),
    )(proj, conv_w8, dqkv)


def _f_gbeta(bt, at, a_log, dt_bias):
    xs = at + dt_bias
    softplus = jnp.maximum(xs, 0.0) + jnp.log(1.0 + jnp.exp(-jnp.abs(xs)))
    return jax.nn.sigmoid(bt), -jnp.exp(a_log) * softplus


def _small(name, fn, ins, out_shapes):
    def body(*refs):
        vals = fn(*[r[...] for r in refs[:len(ins)]])
        for ref, val in zip(refs[len(ins):], vals):
            ref[...] = val.astype(ref.dtype)

    return _call(body, name=name, out_shape=[jax.ShapeDtypeStruct(s, F32) for s in out_shapes],
                 compiler_params=_params())(*ins)


def _dot3(a, b, dims=(((1,), (0,)), ((), ()))):
    ah, bh = a.astype(BF16), b.astype(BF16)
    al, bl = (a - ah.astype(F32)).astype(BF16), (b - bh.astype(F32)).astype(BF16)
    d = lambda u, v: lax.dot_general(u, v, dims, preferred_element_type=F32)
    return d(ah, bh) + (d(ah, bl) + d(al, bh))


def _bdot(a, b, dims=(((1,), (0,)), ((), ()))):
    return lax.dot_general(a.astype(BF16), b.astype(BF16), dims, preferred_element_type=F32)


def _gdn_group(s0, q, k, v, g_row, b_row):
    gs = q.shape[0]
    r = lax.broadcasted_iota(jnp.int32, (gs, gs), 0)
    i = lax.broadcasted_iota(jnp.int32, (gs, gs), 1)
    same = (r // GDN_CHUNK) == (i // GDN_CHUNK)
    g_b = jnp.broadcast_to(g_row, (gs, gs))
    b_b = jnp.broadcast_to(b_row, (gs, gs))
    eye = r == i
    g_col = jnp.sum(jnp.where(eye, g_b, 0.0), axis=1, keepdims=True)
    b_col = jnp.sum(jnp.where(eye, b_b, 0.0), axis=1, keepdims=True)
    gc_col = jnp.sum(jnp.where(same & (i <= r), g_b, 0.0), axis=1, keepdims=True)
    gc_row = jnp.sum(jnp.where(same & (r <= i), jnp.broadcast_to(g_col, (gs, gs)), 0.0), axis=0, keepdims=True)
    gl_col = jnp.sum(jnp.where(same, g_b, 0.0), axis=1, keepdims=True)
    tril = same & (i <= r)
    decay = jnp.where(tril, jnp.exp(jnp.where(tril, gc_col - gc_row, 0.0)), 0.0)
    kk = _bdot(k, k, _NT)
    nmat = jnp.where(same & (i < r), b_col * kk * decay, 0.0)
    am = -nmat
    tinv = jnp.where(eye, 1.0, 0.0) + am
    steps = GDN_CHUNK.bit_length() - 1
    for _ in range(steps - 1):
        am = _dot3(am, am)
        tinv = tinv + _dot3(tinv, am)
    eg = jnp.exp(gc_col)
    w_v = _dot3(tinv, b_col * v)
    w_k = _dot3(tinv, (b_col * eg) * k)
    attn = _bdot(q, k, _NT) * decay
    q_g = q * eg
    k_dec = k * jnp.exp(gl_col - gc_col)
    ridx = lax.broadcasted_iota(jnp.int32, (gs, 1), 0)
    s = s0
    u_all = jnp.zeros_like(v)
    o_inter = jnp.zeros_like(v)
    for c in range(gs // GDN_CHUNK):
        in_chunk = (ridx // GDN_CHUNK) == c
        u = jnp.where(in_chunk, w_v - _bdot(w_k, s), 0.0)
        o_inter = o_inter + jnp.where(in_chunk, _bdot(q_g, s), 0.0)
        u_all = u_all + u
        gl = jnp.sum(jnp.where(ridx == c * GDN_CHUNK, gl_col, 0.0), axis=0, keepdims=True)
        s = jnp.exp(gl) * s + _bdot(k_dec, u, _TN)
    return o_inter + _bdot(attn, u_all), s


def _gdn_specs(heads, ngrp, rev):
    blk = (lambda n: ngrp - 1 - n) if rev else (lambda n: n)
    def col(off):
        return pl.BlockSpec((GDN_GROUP, LANE), lambda h, n: (blk(n), off * heads + h))
    vec = pl.BlockSpec((1, ngrp, GDN_GROUP), lambda h, n: (h, 0, 0))
    state = pl.BlockSpec((1, 1, HEAD_DIM, HEAD_DIM), lambda h, n: (h, blk(n), 0, 0))
    return col, vec, state, blk


def _gdn_fwd(qkv, g3, b3, heads):
    s_len = qkv.shape[0]
    ngrp = s_len // GDN_GROUP
    col, vec, state, _ = _gdn_specs(heads, ngrp, False)

    def body(q_ref, k_ref, v_ref, g_ref, b_ref, o_ref, st_ref, s_scr):
        n = pl.program_id(1)

        @pl.when(n == 0)
        def _():
            s_scr[...] = jnp.zeros_like(s_scr)

        s0 = s_scr[...]
        st_ref[0, 0] = s0
        o, s1 = _gdn_group(s0, q_ref[...], k_ref[...], v_ref[...], g_ref[0, pl.ds(n, 1), :], b_ref[0, pl.ds(n, 1), :])
        o_ref[...] = o
        s_scr[...] = s1

    return _call(
        body,
        name="gdn_fwd",
        grid=(heads, ngrp),
        in_specs=[col(0), col(1), col(2), vec, vec],
        out_specs=[pl.BlockSpec((GDN_GROUP, LANE), lambda h, n: (n, h)), state],
        out_shape=[
            jax.ShapeDtypeStruct((s_len, heads * LANE), F32),
            jax.ShapeDtypeStruct((heads, ngrp, HEAD_DIM, HEAD_DIM), F32),
        ],
        scratch_shapes=[pltpu.VMEM((HEAD_DIM, HEAD_DIM), F32)],
        compiler_params=_params("parallel", "arbitrary"),
    )(qkv, qkv, qkv, g3, b3)


def _gdn_bwd(qkv, g3, b3, states, do, heads):
    s_len = qkv.shape[0]
    ngrp = s_len // GDN_GROUP
    col, vec, state, blk = _gdn_specs(heads, ngrp, True)
    out_col = pl.BlockSpec((GDN_GROUP, LANE), lambda h, n: (blk(n), h))

    def body(q_ref, k_ref, v_ref, g_ref, b_ref, st_ref, do_ref, dq_ref, dk_ref, dv_ref, dg_ref, db_ref, ds_scr):
        n = pl.program_id(1)
        grp = ngrp - 1 - n

        @pl.when(n == 0)
        def _():
            ds_scr[...] = jnp.zeros_like(ds_scr)

        _, vjp = jax.vjp(_gdn_group, st_ref[0, 0], q_ref[...], k_ref[...], v_ref[...],
                         g_ref[0, pl.ds(grp, 1), :], b_ref[0, pl.ds(grp, 1), :])
        ds0, dq, dk, dv, dg, db = vjp((do_ref[...], ds_scr[...]))
        dq_ref[...] = dq
        dk_ref[...] = dk
        dv_ref[...] = dv
        dg_ref[0, pl.ds(grp, 1), :] = dg
        db_ref[0, pl.ds(grp, 1), :] = db
        ds_scr[...] = ds0

    shp = jax.ShapeDtypeStruct((s_len, heads * LANE), F32)
    vshp = jax.ShapeDtypeStruct(g3.shape, F32)
    return _call(
        body,
        name="gdn_bwd",
        grid=(heads, ngrp),
        in_specs=[col(0), col(1), col(2), vec, vec, state, out_col],
        out_specs=[out_col, out_col, out_col, vec, vec],
        out_shape=[shp, shp, shp, vshp, vshp],
        scratch_shapes=[pltpu.VMEM((HEAD_DIM, HEAD_DIM), F32)],
        compiler_params=_params("parallel", "arbitrary"),
    )(qkv, qkv, qkv, g3, b3, states, do)


def _sum_adam(name, parts, w, m, v, tr):
    n, rows, cols = parts.shape

    def body(p_ref, w_ref, m_ref, v_ref, g_out, d_out, m_out, v_out):
        g = p_ref[0].astype(F32)
        for s in range(1, n):
            g = g + p_ref[s].astype(F32)
        m_new = ADAM_B1 * m_ref[...] + (1.0 - ADAM_B1) * g
        v_new = ADAM_B2 * v_ref[...] + (1.0 - ADAM_B2) * (g * g)
        m_hat = m_new / (1.0 - ADAM_B1 ** ADAM_STEP)
        v_hat = v_new / (1.0 - ADAM_B2 ** ADAM_STEP)
        g_out[...] = g
        d_out[...] = -ADAM_LR * (m_hat / (jnp.sqrt(v_hat) + ADAM_EPS) + ADAM_WD * w_ref[...])
        m_out[...] = m_new
        v_out[...] = v_new

    mat = pl.BlockSpec((tr, cols), lambda i: (i, 0))
    shp = jax.ShapeDtypeStruct((rows, cols), F32)
    return _call(
        body,
        name=name,
        grid=(rows // tr,),
        in_specs=[pl.BlockSpec((n, tr, cols), lambda i: (0, i, 0)), mat, mat, mat],
        out_specs=[mat, mat, mat, mat],
        out_shape=[shp, shp, shp, shp],
        compiler_params=_params("parallel"),
    )(parts, w, m, v)


def _sum_parts(name, parts):
    n = parts.shape[0]

    def fn(p):
        g = p[0]
        for s in range(1, n):
            g = g + p[s]
        return (g,)

    return _small(name, fn, [parts], [parts.shape[1:]])[0]


def _pad_rows(a, mult):
    r = (-a.shape[0]) % mult
    return a if r == 0 else jnp.concatenate([a, jnp.zeros((r,) + a.shape[1:], a.dtype)], axis=0)


def _pad_lanes(a, width):
    return jnp.concatenate([a, jnp.zeros(a.shape[:-1] + (width - a.shape[-1],), a.dtype)], axis=-1)


def _pack(pieces, width, mult):
    return _pad_rows(jnp.concatenate([p.reshape(-1, width) for p in pieces], axis=0), mult)


def _unpack(packed, shapes, width):
    out, r0 = [], 0
    for shp in shapes:
        size = 1
        for d in shp:
            size *= d
        nr = size // width
        out.append(packed[..., r0:r0 + nr, :].reshape(packed.shape[:-2] + tuple(shp)))
        r0 += nr
    return out


def kernel(x, c, w_mod, b_mod, norm1_w, w_in, q_norm_w, k_norm_w, conv_w, a_log, dt_bias, o_norm_w, p_a, p_b, w_out, norm2_w, w_gate, w_up, w_down, loss_target, m_w_mod, m_b_mod, m_norm1_w, m_w_in, m_q_norm_w, m_k_norm_w, m_conv_w, m_a_log, m_dt_bias, m_o_norm_w, m_p_a, m_p_b, m_w_out, m_norm2_w, m_w_gate, m_w_up, m_w_down, v_w_mod, v_b_mod, v_norm1_w, v_w_in, v_q_norm_w, v_k_norm_w, v_conv_w, v_a_log, v_dt_bias, v_o_norm_w, v_p_a, v_p_b, v_w_out, v_norm2_w, v_w_gate, v_w_up, v_w_down):
    s_len, d = x.shape[1], x.shape[2]
    heads = a_log.shape[1]
    dh = heads * HEAD_DIM
    f = w_down.shape[1] * N_DEV
    din_loc = w_in.shape[2]
    me = _my_id()
    x2, tgt = x[0], loss_target[0]

    big = [w_in, p_a, p_b, w_out, w_gate, w_up, w_down]
    big_shapes = [t.shape[1:] for t in big]
    wg = _exchange("gather_weights", _pack([t[0].astype(BF16) for t in big], d, LANE), False)
    g_in, g_pa, g_pb, g_out, g_gate, g_up, g_down = _unpack(wg, big_shapes, d)
    cols = lambda t: t.transpose(1, 0, 2).reshape(t.shape[1], -1)
    rows = lambda t: t.reshape(-1, t.shape[2])
    w_in_g = cols(g_in)
    o_ba = 3 * dh + 3 * dh + dh
    din = w_in_g.shape[1]
    n_perm = o_ba + 2 * d + LANE
    w_in_p = jnp.concatenate(
        [w_in_g[:, :o_ba], w_in_g[:, o_ba + 2 * heads:], w_in_g[:, o_ba:o_ba + 2 * heads],
         jnp.zeros((d, LANE - 2 * heads), BF16)], axis=1)
    o_qkvb, o_z, o_ga, o_bad = 3 * dh, 6 * dh, 7 * dh, 7 * dh + 2 * d
    w_pa, w_pb, w_o, w_dn = rows(g_pa), rows(g_pb), rows(g_out), rows(g_down)
    w_gu = jnp.concatenate([cols(g_gate), cols(g_up)], axis=1)

    c_all = _pad_rows(_exchange("gather_c", c, False).reshape(N_DEV, d), LANE)
    mod_part = _mm("mod_fwd", c_all, w_mod[0], a_fn=_silu, tk=d)[:N_DEV]
    mod_all = _exchange("gather_mod", mod_part, False)
    mod_me = lax.dynamic_index_in_dim(mod_all, me, axis=1, keepdims=False).reshape(1, 6 * d) + b_mod
    sh1, sc1, gt1, sh2, sc2, gt2 = [mod_me[:, j * d:(j + 1) * d] for j in range(6)]

    tm = min(128, s_len)
    th = min(1024, s_len)
    (u1,) = _seg("pre1_fwd", lambda a, nw, sc, sh: (_f_mod(a, nw, sc, sh),),
                 [_row(x2), _par(norm1_w), _par(sc1), _par(sh1)], [(d, BF16, "row", d, 0, 0)], s_len, tm)
    proj = _mm("proj", u1, w_in_p, tm=1024, tn=640, tk=d)

    hb = dh // LANE
    qn, kn, vb = _seg(
        "qk_fwd", lambda q, k, v, qw, kw: _f_qk(q, k, qw, kw) + (v,),
        [_row(proj, LANE, 0, 1), _row(proj, LANE, hb, 1), _row(proj, LANE, 2 * hb, 1), _par(q_norm_w), _par(k_norm_w)],
        [(dh, BF16, "row", LANE, 0, 1)] * 3, s_len, th, nh=hb)
    bq = min(512, s_len)
    o_a = _sb_fwd(qn, kn, vb, heads, bq)

    conv_all = _exchange("gather_conv", conv_w[0], False)
    conv_g = _pad_rows(conv_all.transpose(1, 0, 2).reshape(CONV_TAPS, 3 * dh), 8)
    tr = min(512, s_len)
    qkv_b = _bprep_fwd(proj, o_qkvb, conv_g, heads, tr)
    ba_t = proj[:, o_bad:o_bad + 2 * heads].T
    a_col, dt_col = a_log.reshape(heads, 1), dt_bias.reshape(heads, 1)
    beta_t, g_t = _small("gbeta_fwd", _f_gbeta, [ba_t[:heads], ba_t[heads:], a_col, dt_col], [(heads, s_len)] * 2)
    ngrp = s_len // GDN_GROUP
    g3, b3 = g_t.reshape(heads, ngrp, GDN_GROUP), beta_t.reshape(heads, ngrp, GDN_GROUP)
    o_b, states = _gdn_fwd(qkv_b, g3, b3, heads)
    zoff = o_z // LANE
    (ob2,) = _seg("post_fwd", lambda ob, z, ow: (_f_post(ob, z, ow),),
                  [_row(o_b, LANE, 0, 1), _row(proj, LANE, zoff, 1), _par(o_norm_w)],
                  [(dh, BF16, "row", LANE, 0, 1)], s_len, th, nh=hb)

    ya = _mm("ya", o_a, w_pa, tm=1024, tk=dh)
    yb = _mm("yb", ob2, w_pb, tm=1024, tk=dh)
    goff = o_ga // d
    gate_ins = [_row(proj, d, goff, 0), _row(proj, d, goff + 1, 0)]
    (merged,) = _seg("merge_fwd", lambda a, b, ga, gb: (_f_merge(a, b, ga, gb),),
                     [_row(ya), _row(yb)] + gate_ins, [(d, BF16, "row", d, 0, 0)], s_len, tm)
    t_out = _mm("attn_out", merged, w_o, tm=1024, tk=d)
    res_par = [_par(gt1), _par(norm2_w), _par(sc2), _par(sh2)]
    h1, u2 = _seg("res1_fwd", _f_res, [_row(x2), _row(t_out)] + res_par,
                  [(d, F32, "row", d, 0, 0), (d, BF16, "row", d, 0, 0)], s_len, tm)
    gu = _mm("ffn_in", u2, w_gu, tm=1024, tk=d)
    tf = 128
    (ff,) = _seg("ff_fwd", lambda a, b: (_f_ff(a, b),), [_row(gu, f, 0, 0), _row(gu, f, 1, 0)],
                 [(f, BF16, "row", f, 0, 0)], s_len, tf)
    dn = _mm("ffn_out", ff, w_dn, tm=1024, tk=512)

    def f_loss(hh, dd, g2, tg):
        err = hh + g2 * dd - tg
        dh2 = err * (1.0 / d)
        return (dh2, g2 * dh2, jnp.sum(err * err, axis=0, keepdims=True), jnp.sum(dh2 * dd, axis=0, keepdims=True))

    dh2, ddn, lsq, dgt2 = _seg(
        "loss", f_loss, [_row(h1), _row(dn), _par(gt2), _row(tgt)],
        [(d, F32, "row", d, 0, 0), (d, BF16, "row", d, 0, 0), (d, F32, "acc", d, 0, 0), (d, F32, "acc", d, 0, 0)],
        s_len, tm)
    loss = lax.psum(0.5 * jnp.sum(lsq) / d, MESH_AXES)

    dff = _mm("d_ff", ddn, w_dn, tb=True, tm=1024, tk=d)
    dw_down = _mm("dw_down", ff, ddn, ta=True, out_dtype=BF16, tk=1024)

    def f_ff_bwd(a, b, dy):
        _, vjp = jax.vjp(_f_ff, a, b)
        da, db = vjp(dy)
        return (jnp.concatenate([da, db], axis=1),)

    (dgu,) = _seg("ff_bwd", f_ff_bwd, [_row(gu, f, 0, 0), _row(gu, f, 1, 0), _row(dff)],
                  [(2 * f, BF16, "row", 2 * f, 0, 0)], s_len, tf)
    du2 = _mm("d_u2", dgu, w_gu, tb=True, tm=1024, tk=1024)
    dw_gu = _mm("dw_gu", u2, dgu, ta=True, out_dtype=BF16, tk=1024)

    def f_res_bwd(a, t, g, nw, sc, sh, dhd, du):
        _, vjp = jax.vjp(_f_res, a, t, g, nw, sc, sh)
        da, dt_, dg, dnw, dsc, dsh = vjp((dhd, du))
        return da, dt_, dg, dnw, dsc, dsh

    acc_d = (d, F32, "acc", d, 0, 0)
    dh1, dt_out, dgt1, dnorm2, dsc2, dsh2 = _seg(
        "res1_bwd", f_res_bwd, [_row(x2), _row(t_out)] + res_par + [_row(dh2), _row(du2)],
        [(d, F32, "row", d, 0, 0), (d, BF16, "row", d, 0, 0), acc_d, acc_d, acc_d, acc_d], s_len, tm)

    dmerged = _mm("d_merged", dt_out, w_o, tb=True, tm=1024, tk=d)
    dw_out = _mm("dw_out", merged, dt_out, ta=True, out_dtype=BF16, tk=1024)

    def f_merge_bwd(a, b, ga, gb, dy):
        _, vjp = jax.vjp(_f_merge, a, b, ga, gb)
        da, db, dga, dgb = vjp(dy)
        return da, db, jnp.concatenate([dga, dgb], axis=1)

    dya, dyb, dgates = _seg("merge_bwd", f_merge_bwd, [_row(ya), _row(yb)] + gate_ins + [_row(dmerged)],
                            [(d, BF16, "row", d, 0, 0), (d, BF16, "row", d, 0, 0), (2 * d, BF16, "row", 2 * d, 0, 0)],
                            s_len, tm)
    do_a = _mm("d_oa", dya, w_pa, tb=True, tm=1024, tk=d)
    dw_pa = _mm("dw_pa", o_a, dya, ta=True, out_dtype=BF16, tk=1024)
    dob2 = _mm("d_ob2", dyb, w_pb, tb=True, tm=1024, tk=d)
    dw_pb = _mm("dw_pb", ob2, dyb, ta=True, out_dtype=BF16, tk=1024)

    def f_post_bwd(ob, z, ow, dy):
        _, vjp = jax.vjp(_f_post, ob, z, ow)
        return vjp(dy)

    acc_h = (LANE, F32, "acc", LANE, 0, 0)
    d_ob, dz, d_onorm = _seg(
        "post_bwd", f_post_bwd,
        [_row(o_b, LANE, 0, 1), _row(proj, LANE, zoff, 1), _par(o_norm_w), _row(dob2, LANE, 0, 1)],
        [(dh, F32, "row", LANE, 0, 1), (dh, BF16, "row", LANE, 0, 1), acc_h], s_len, th, nh=hb)
    dqb, dkb, dvb, dg3, db3 = _gdn_bwd(qkv_b, g3, b3, states, d_ob, heads)
    dqkv_n = jnp.concatenate([dqb, dkb, dvb], axis=1)
    dqkv_pre, dconv8 = _bprep_bwd(proj, o_qkvb, conv_g, dqkv_n, heads, tr)

    def f_gbeta_bwd(bt, at, al, dtb, dbeta, dg):
        _, vjp = jax.vjp(_f_gbeta, bt, at, al, dtb)
        return vjp((dbeta, dg))

    dbt, dat, da_log, ddt = _small(
        "gbeta_bwd", f_gbeta_bwd,
        [ba_t[:heads], ba_t[heads:], a_col, dt_col, db3.reshape(heads, s_len), dg3.reshape(heads, s_len)],
        [(heads, s_len), (heads, s_len), (heads, 1), (heads, 1)])
    dba = _pad_lanes(jnp.concatenate([dbt, dat], axis=0).T, LANE).astype(BF16)

    dqn, dkn, dva = _sb_bwd(qn, kn, vb, do_a, heads, bq)

    def f_qk_bwd(q, k, qw, kw, dq, dk):
        _, vjp = jax.vjp(_f_qk, q, k, qw, kw)
        return vjp((dq, dk))

    dqa, dka, d_qnorm, d_knorm = _seg(
        "qk_bwd", f_qk_bwd,
        [_row(proj, LANE, 0, 1), _row(proj, LANE, hb, 1), _par(q_norm_w), _par(k_norm_w),
         _row(dqn, LANE, 0, 1), _row(dkn, LANE, 0, 1)],
        [(dh, BF16, "row", LANE, 0, 1), (dh, BF16, "row", LANE, 0, 1), acc_h, acc_h], s_len, th, nh=hb)

    dproj = jnp.concatenate([dqa, dka, dva.astype(BF16), dqkv_pre, dz, dgates, dba], axis=1)
    du1 = _mm("d_u1", dproj, w_in_p, tb=True, tm=1024, tk=640)
    dw_in_p = _mm("dw_in", u1, dproj, ta=True, out_dtype=BF16, tn=640, tk=1024)

    def f_pre_bwd(a, nw, sc, sh, du, dres):
        _, vjp = jax.vjp(_f_mod, a, nw, sc, sh)
        da, dnw, dsc, dsh = vjp(du)
        return da + dres, dnw, dsc, dsh

    dx, dnorm1, dsc1, dsh1 = _seg(
        "pre1_bwd", f_pre_bwd, [_row(x2), _par(norm1_w), _par(sc1), _par(sh1), _row(du1), _row(dh1)],
        [(d, F32, "row", d, 0, 0), acc_d, acc_d, acc_d], s_len, tm)

    dmod = jnp.concatenate([dsh1, dsc1, dgt1, dsh2, dsc2, dgt2], axis=1)
    lanes = lambda t: _pad_lanes(t.reshape(1, -1), LANE)
    small = [dmod, dnorm1, d_qnorm, d_knorm, lanes(da_log), lanes(ddt), d_onorm, dnorm2, dconv8[:CONV_TAPS]]
    small_shapes = [t.shape for t in small]
    small_all = _exchange("gather_small", _pack(small, LANE, 8), False)
    small_sum = _sum_parts("sum_small", small_all)
    g_bmod, g_n1, g_qn, g_kn, g_al, g_dt, g_on, g_n2, g_conv = _unpack(small_sum, small_shapes, LANE)
    ncv = conv_w.shape[2]
    g_conv_me = lax.dynamic_slice_in_dim(g_conv, me * ncv, ncv, axis=1)

    nmod = w_mod.shape[2]
    dmod_all = _unpack(small_all, small_shapes[:1], LANE)[0].reshape(N_DEV, 6 * d)
    dmod_cols = _pad_rows(lax.dynamic_slice_in_dim(dmod_all, me * nmod, nmod, axis=1), LANE)
    g_wmod = _mm("dw_mod", c_all, dmod_cols, ta=True, a_fn=_silu, tk=LANE)

    dw_in_g = jnp.concatenate([dw_in_p[:, :o_ba], dw_in_p[:, o_bad:o_bad + 2 * heads], dw_in_p[:, o_ga:o_ga + 2 * d]], axis=1)
    to_cols = lambda t: t.reshape(t.shape[0], N_DEV, -1).transpose(1, 0, 2)
    to_rows = lambda t: t.reshape(N_DEV, -1, t.shape[1])
    parts = [to_cols(dw_in_g), to_rows(dw_pa), to_rows(dw_pb), to_rows(dw_out),
             to_cols(dw_gu[:, :f]), to_cols(dw_gu[:, f:]), to_rows(dw_down)]
    send = jnp.concatenate([p.reshape(N_DEV, -1, d) for p in parts], axis=1)
    pad = (-send.shape[1]) % LANE
    if pad:
        send = jnp.concatenate([send, jnp.zeros((N_DEV, pad, d), BF16)], axis=1)
    recv = _exchange("scatter_grads", send, True)

    def adam_packed(name, parts_, ws, ms, vs, width, mult, tr_):
        outs = _sum_adam(name, parts_, _pack(ws, width, mult), _pack(ms, width, mult), _pack(vs, width, mult), tr_)
        shapes = [t.shape for t in ws]
        return [_unpack(o, shapes, width) for o in outs]

    big_m = [m_w_in, m_p_a, m_p_b, m_w_out, m_w_gate, m_w_up, m_w_down]
    big_v = [v_w_in, v_p_a, v_p_b, v_w_out, v_w_gate, v_w_up, v_w_down]
    gb, db_, mb, vb_ = adam_packed("adam_big", recv, big, big_m, big_v, d, LANE, LANE)

    gm, dm_, mm_, vm_ = adam_packed("adam_mod", g_wmod[None], [w_mod], [m_w_mod], [v_w_mod], nmod, 8, _tile(d, LANE))

    sm_w = [b_mod, norm1_w, q_norm_w, k_norm_w, lanes(a_log), lanes(dt_bias), o_norm_w, norm2_w, conv_w[0]]
    sm_m = [m_b_mod, m_norm1_w, m_q_norm_w, m_k_norm_w, lanes(m_a_log), lanes(m_dt_bias), m_o_norm_w, m_norm2_w, m_conv_w[0]]
    sm_v = [v_b_mod, v_norm1_w, v_q_norm_w, v_k_norm_w, lanes(v_a_log), lanes(v_dt_bias), v_o_norm_w, v_norm2_w, v_conv_w[0]]
    sm_g = [g_bmod, g_n1, g_qn, g_kn, g_al, g_dt, g_on, g_n2, g_conv_me]
    g_pack = _pack(sm_g, LANE, 8)
    gs_, ds_, ms_, vs_ = adam_packed("adam_small", g_pack[None], sm_w, sm_m, sm_v, LANE, 8, g_pack.shape[0])

    def assemble(big_l, mod_l, small_l):
        s_bmod, s_n1, s_qn, s_kn, s_al, s_dt, s_on, s_n2, s_conv = small_l
        b_in, b_pa, b_pb, b_out, b_gate, b_up, b_down = big_l
        return [mod_l[0], s_bmod, s_n1, b_in, s_qn, s_kn, s_conv[None], s_al[:, :heads], s_dt[:, :heads], s_on,
                b_pa, b_pb, b_out, s_n2, b_gate, b_up, b_down]

    outs = [loss, dx[None]]
    for big_l, mod_l, small_l in ((gb, gm, gs_), (db_, dm_, ds_), (mb, mm_, ms_), (vb_, vm_, vs_)):
        outs += assemble(big_l, mod_l, small_l)
    return tuple(outs)
```

```python
import jax
import jax.numpy as jnp
from jax import lax
from jax.experimental import pallas as pl
from jax.experimental.pallas import tpu as pltpu

F32, BF16 = jnp.float32, jnp.bfloat16
EPS = 1e-6
HEAD_DIM = 128
GDN_CHUNK = 64
GDN_GROUP = 256
CONV_TAPS = 4
N_DEV = 8
MESH_AXES = ("x", "y", "c")
ADAM_LR, ADAM_B1, ADAM_B2, ADAM_EPS, ADAM_WD, ADAM_STEP = 0.001, 0.9, 0.999, 1e-08, 0.01, 10
VMEM_LIMIT_BYTES = 56 * 1024 * 1024
LANE = 128
MESH_ID = pl.DeviceIdType.MESH


def _call(body, **kw):
    return pl.pallas_call(body, **kw)


def _params(*sem):
    return pltpu.CompilerParams(dimension_semantics=sem or None, vmem_limit_bytes=VMEM_LIMIT_BYTES)


def _tile(n, target):
    t = (min(n, target) // LANE) * LANE
    while t >= LANE:
        if n % t == 0:
            return t
        t -= LANE
    return n


def _silu(x):
    return x * jax.nn.sigmoid(x)


def _my_id():
    return 4 * lax.axis_index("x") + 2 * lax.axis_index("y") + lax.axis_index("c")


def _exchange(name, src, scatter):
    blk = src.shape[1:] if scatter else src.shape

    def body(src_ref, out_ref, send_sems, recv_sems, local_sem):
        x, y, c = lax.axis_index("x"), lax.axis_index("y"), lax.axis_index("c")
        me = 4 * x + 2 * y + c

        def peer(k):
            kx, ky, kc = (k >> 2) & 1, (k >> 1) & 1, k & 1
            px, py, pc = x ^ kx, y ^ ky, c ^ kc
            return (px, py, pc), 4 * px + 2 * py + pc

        def copy(k):
            dev, pid = peer(k)
            return pltpu.make_async_remote_copy(
                src_ref=src_ref.at[pid] if scatter else src_ref,
                dst_ref=out_ref.at[me],
                send_sem=send_sems.at[k - 1],
                recv_sem=recv_sems.at[k - 1],
                device_id=dev,
                device_id_type=MESH_ID,
            )

        def arrival(k):
            dev, pid = peer(k)
            return pltpu.make_async_remote_copy(
                src_ref=src_ref.at[pid] if scatter else src_ref,
                dst_ref=out_ref.at[pid],
                send_sem=send_sems.at[k - 1],
                recv_sem=recv_sems.at[k - 1],
                device_id=dev,
                device_id_type=MESH_ID,
            )

        mine = pltpu.make_async_copy(src_ref.at[me] if scatter else src_ref, out_ref.at[me], local_sem)
        mine.start()
        for k in range(1, N_DEV):
            copy(k).start()
        for k in range(1, N_DEV):
            arrival(k).wait_recv()
        for k in range(1, N_DEV):
            copy(k).wait_send()
        mine.wait()

    return _call(
        body,
        name=name,
        out_shape=jax.ShapeDtypeStruct((N_DEV,) + tuple(blk), src.dtype),
        in_specs=[pl.BlockSpec(memory_space=pl.ANY)],
        out_specs=pl.BlockSpec(memory_space=pl.ANY),
        scratch_shapes=[
            pltpu.SemaphoreType.DMA((N_DEV - 1,)),
            pltpu.SemaphoreType.DMA((N_DEV - 1,)),
            pltpu.SemaphoreType.DMA,
        ],
    )(src)


def _mm(name, a, b, *, ta=False, tb=False, out_dtype=F32, a_fn=None, tm=512, tn=512, tk=512):
    m, kdim = (a.shape[1], a.shape[0]) if ta else a.shape
    n = b.shape[0] if tb else b.shape[1]
    assert kdim == (b.shape[1] if tb else b.shape[0]), (a.shape, b.shape, ta, tb)
    tm, tn, tk = _tile(m, tm), _tile(n, tn), _tile(kdim, tk)
    nk = kdim // tk
    a_spec = pl.BlockSpec((tk, tm), lambda i, j, k: (k, i)) if ta else pl.BlockSpec((tm, tk), lambda i, j, k: (i, k))
    b_spec = pl.BlockSpec((tn, tk), lambda i, j, k: (j, k)) if tb else pl.BlockSpec((tk, tn), lambda i, j, k: (k, j))
    dims = (((0 if ta else 1,), (1 if tb else 0,)), ((), ()))

    def body(a_ref, b_ref, o_ref, acc_ref):
        k = pl.program_id(2)
        av = a_ref[...]
        if a_fn is not None:
            av = a_fn(av.astype(F32))
        p = lax.dot_general(av.astype(BF16), b_ref[...].astype(BF16), dims, preferred_element_type=F32)
        if nk == 1:
            o_ref[...] = p.astype(out_dtype)
        else:
            @pl.when(k == 0)
            def _():
                acc_ref[...] = p

            @pl.when(k > 0)
            def _():
                acc_ref[...] += p

            @pl.when(k == nk - 1)
            def _():
                o_ref[...] = acc_ref[...].astype(out_dtype)

    return _call(
        body,
        name=name,
        grid=(m // tm, n // tn, nk),
        in_specs=[a_spec, b_spec],
        out_specs=pl.BlockSpec((tm, tn), lambda i, j, k: (i, j)),
        out_shape=jax.ShapeDtypeStruct((m, n), out_dtype),
        scratch_shapes=[pltpu.VMEM((tm, tn) if nk > 1 else (8, LANE), F32)],
        compiler_params=_params("parallel", "parallel", "arbitrary"),
    )(a, b)


def _row(arr, w=None, off=0, stride=0):
    return (arr, "row", arr.shape[1] if w is None else w, off, stride)


def _par(arr, w=None, off=0, stride=0):
    return (arr, "par", arr.shape[1] if w is None else w, off, stride)


def _seg(name, fn, ins, outs, rows, tm, nh=1):
    nrow = rows // tm

    def spec(kind, w, off, stride):
        if kind == "row":
            return pl.BlockSpec((tm, w), lambda h, i: (i, off + stride * h))
        return pl.BlockSpec((1, w), lambda h, i: (0, off + stride * h))

    in_specs = [spec(kind, w, off, stride) for (_, kind, w, off, stride) in ins]
    out_specs = [spec("row" if kind == "row" else "par", w, off, stride) for (_, _, kind, w, off, stride) in outs]
    out_shape = [
        jax.ShapeDtypeStruct((rows if kind == "row" else 1, ncols), dt) for (ncols, dt, kind, _, _, _) in outs
    ]
    n_in = len(ins)

    def body(*refs):
        h, i = pl.program_id(0), pl.program_id(1)
        vals = fn(*[r[...] for r in refs[:n_in]])
        for (_, dt, kind, _, _, stride), ref, val in zip(outs, refs[n_in:], vals):
            if kind == "row":
                ref[...] = val.astype(dt)
            else:
                first = (i == 0) if stride != 0 else jnp.logical_and(i == 0, h == 0)

                @pl.when(first)
                def _(ref=ref, val=val):
                    ref[...] = val.astype(F32)

                @pl.when(jnp.logical_not(first))
                def _(ref=ref, val=val):
                    ref[...] += val.astype(F32)

    res = _call(
        body,
        name=name,
        grid=(nh, nrow),
        in_specs=in_specs,
        out_specs=out_specs,
        out_shape=out_shape,
        compiler_params=_params("arbitrary", "arbitrary"),
    )(*[t[0] for t in ins])
    return res


def _rmsn(x, w):
    return x * lax.rsqrt(jnp.mean(x * x, axis=-1, keepdims=True) + EPS) * w


def _f_mod(x, nw, sc, sh):
    return _rmsn(x, nw) * (1.0 + sc) + sh


def _f_res(x, t, g, nw, sc, sh):
    h = x + g * t
    return h, _f_mod(h, nw, sc, sh)


def _f_ff(gg, uu):
    return _silu(gg) * uu


def _f_merge(ya, yb, ga, gb):
    return jax.nn.sigmoid(ga) * ya + jax.nn.sigmoid(gb) * yb


def _f_qk(q, k, qw, kw):
    return _rmsn(q, qw), _rmsn(k, kw)


def _f_post(ob, z, ow):
    return _rmsn(ob, ow) * _silu(z)


def _f32(*xs):
    return [x.astype(F32) for x in xs]


def _log_sigmoid(z):
    return jnp.minimum(z, 0.0) - jnp.log(1.0 + jnp.exp(-jnp.abs(z)))


def _dot2(x, tri):
    hi = x.astype(BF16)
    lo = (x - hi.astype(F32)).astype(BF16)
    return jnp.dot(hi, tri, preferred_element_type=F32) + jnp.dot(lo, tri, preferred_element_type=F32)


def _tri(strict):
    j = lax.broadcasted_iota(jnp.int32, (LANE, LANE), 0)
    s = lax.broadcasted_iota(jnp.int32, (LANE, LANE), 1)
    return ((j > s) if strict else (j >= s)).astype(BF16)


SB_SLAB = 512


def _sb_slabs(bq, slab):
    return [slice(r * slab, (r + 1) * slab) for r in range(bq // slab)]


def _sb_diagonal(jj, n_slabs, slab):
    out = []
    for r in range(n_slabs):
        if jj * LANE >= (r + 1) * slab:
            continue
        out.append((r, "full" if (jj + 1) * LANE <= r * slab else "mask"))
    return out


def _sb_mask(r, jj, slab):
    t = r * slab + lax.broadcasted_iota(jnp.int32, (slab, LANE), 0)
    s = jj * LANE + lax.broadcasted_iota(jnp.int32, (slab, LANE), 1)
    return s < t


_NT = (((1,), (1,)), ((), ()))
_TN = (((0,), (0,)), ((), ()))


def _sb_fwd(qn, kn, vb, heads, bq):
    s_len = qn.shape[0]
    scale = HEAD_DIM ** -0.5
    sub = bq // LANE
    slab = min(SB_SLAB, bq)

    def body(q_ref, k_ref, v_ref, o_ref):
        i = pl.program_id(1)
        tri = _tri(True)
        slabs = _sb_slabs(bq, slab)
        qs = [q_ref[rw, :] for rw in slabs]

        def tile(q, k, v, cl, acc, mask):
            z = lax.dot_general(q, k, _NT, preferred_element_type=F32) * scale
            ls = _log_sigmoid(z)
            lm = ls - z
            if mask is not None:
                lm = jnp.where(mask, lm, 0.0)
            w = jnp.exp(ls + (_dot2(lm, tri) + cl))
            if mask is not None:
                w = jnp.where(mask, w, 0.0)
            return cl + jnp.sum(lm, axis=1, keepdims=True), acc + _dot2(w, v)

        state = [(jnp.zeros((slab, 1), F32), jnp.zeros((slab, LANE), F32)) for _ in slabs]
        for jj in reversed(range(sub)):
            j0 = pl.multiple_of(i * bq + jj * LANE, LANE)
            k, v = k_ref[pl.ds(j0, LANE), :], v_ref[pl.ds(j0, LANE), :]
            for r, kind in _sb_diagonal(jj, len(slabs), slab):
                state[r] = tile(qs[r], k, v, *state[r], _sb_mask(r, jj, slab) if kind == "mask" else None)

        def step(t, carry):
            j0 = pl.multiple_of((i * sub - 1 - t) * LANE, LANE)
            k, v = k_ref[pl.ds(j0, LANE), :], v_ref[pl.ds(j0, LANE), :]
            return tuple(tile(qs[r], k, v, *carry[r], None) for r in range(len(slabs)))

        state = lax.fori_loop(0, i * sub, step, tuple(state))
        for r, rw in enumerate(slabs):
            o_ref[rw, :] = state[r][1]

    return _call(
        body,
        name="sb_fwd",
        grid=(heads, s_len // bq),
        in_specs=[
            pl.BlockSpec((bq, LANE), lambda h, i: (i, h)),
            pl.BlockSpec((s_len, LANE), lambda h, i: (0, h)),
            pl.BlockSpec((s_len, LANE), lambda h, i: (0, h)),
        ],
        out_specs=pl.BlockSpec((bq, LANE), lambda h, i: (i, h)),
        out_shape=jax.ShapeDtypeStruct(qn.shape, F32),
        compiler_params=_params("parallel", "arbitrary"),
    )(qn, kn, vb)


def _sb_bwd(qn, kn, vb, o, do, heads, bq):
    s_len = qn.shape[0]
    scale = HEAD_DIM ** -0.5
    sub = bq // LANE
    slab = min(SB_SLAB, bq)

    def body(q_ref, k_ref, v_ref, o_ref, do_ref, dq_ref, dk_ref, dv_ref):
        i = pl.program_id(1)

        @pl.when(i == 0)
        def _():
            dk_ref[...] = jnp.zeros_like(dk_ref)
            dv_ref[...] = jnp.zeros_like(dv_ref)

        tri, tri_inc = _tri(True), _tri(False)
        slabs = _sb_slabs(bq, slab)
        qs = [q_ref[rw, :] for rw in slabs]
        dobs = [do_ref[rw, :].astype(BF16) for rw in slabs]
        etots = [jnp.sum(dobs[r].astype(F32) * o_ref[rw, :], axis=1, keepdims=True) for r, rw in enumerate(slabs)]

        def tile(r, k, v, cl, ce, dq, mask):
            z = lax.dot_general(qs[r], k, _NT, preferred_element_type=F32) * scale
            ls = _log_sigmoid(z)
            lm = ls - z
            if mask is not None:
                lm = jnp.where(mask, lm, 0.0)
            a = jnp.exp(ls + (_dot2(lm, tri) + cl))
            if mask is not None:
                a = jnp.where(mask, a, 0.0)
            e = a * lax.dot_general(dobs[r], v, _NT, preferred_element_type=F32)
            before = etots[r] - (_dot2(e, tri_inc) + ce)
            sig = jnp.exp(ls)
            dz = (e * (1.0 - sig) - sig * before) * scale
            if mask is not None:
                dz = jnp.where(mask, dz, 0.0)
            dzb = dz.astype(BF16)
            dq = dq + jnp.dot(dzb, k, preferred_element_type=F32)
            carry = (cl + jnp.sum(lm, axis=1, keepdims=True), ce + jnp.sum(e, axis=1, keepdims=True), dq)
            return carry, dzb, a.astype(BF16)

        def scatter_kv(j0, dzbs, abs_, which):
            cat = lambda xs: xs[0] if len(xs) == 1 else jnp.concatenate(xs, axis=0)
            qcat, docat = cat([qs[r] for r in which]), cat([dobs[r] for r in which])
            dk_ref[pl.ds(j0, LANE), :] += lax.dot_general(cat(dzbs), qcat, _TN, preferred_element_type=F32)
            dv_ref[pl.ds(j0, LANE), :] += lax.dot_general(cat(abs_), docat, _TN, preferred_element_type=F32)

        zero = jnp.zeros((slab, 1), F32)
        state = [(zero, zero, jnp.zeros((slab, LANE), F32)) for _ in slabs]
        for jj in reversed(range(sub)):
            j0 = pl.multiple_of(i * bq + jj * LANE, LANE)
            k, v = k_ref[pl.ds(j0, LANE), :], v_ref[pl.ds(j0, LANE), :]
            dzbs, abs_, which = [], [], []
            for r, kind in _sb_diagonal(jj, len(slabs), slab):
                state[r], dzb, ab = tile(r, k, v, *state[r], _sb_mask(r, jj, slab) if kind == "mask" else None)
                dzbs.append(dzb)
                abs_.append(ab)
                which.append(r)
            scatter_kv(j0, dzbs, abs_, which)

        def step(t, carry):
            j0 = pl.multiple_of((i * sub - 1 - t) * LANE, LANE)
            k, v = k_ref[pl.ds(j0, LANE), :], v_ref[pl.ds(j0, LANE), :]
            outs = [tile(r, k, v, *carry[r], None) for r in range(len(slabs))]
            scatter_kv(j0, [o[1] for o in outs], [o[2] for o in outs], list(range(len(slabs))))
            return tuple(o[0] for o in outs)

        state = lax.fori_loop(0, i * sub, step, tuple(state))
        for r, rw in enumerate(slabs):
            dq_ref[rw, :] = state[r][2]

    tile_spec = pl.BlockSpec((bq, LANE), lambda h, i: (i, h))
    full = pl.BlockSpec((s_len, LANE), lambda h, i: (0, h))
    shp = jax.ShapeDtypeStruct(qn.shape, F32)
    return _call(
        body,
        name="sb_bwd",
        grid=(heads, s_len // bq),
        in_specs=[tile_spec, full, full, tile_spec, tile_spec],
        out_specs=[tile_spec, full, full],
        out_shape=[shp, shp, shp],
        compiler_params=_params("parallel", "arbitrary"),
    )(qn, kn, vb, o, do)


def _shift_down(cur, halo, k):
    if k == 0:
        return cur
    r = pltpu.roll(cur, k, 0)
    p = pltpu.roll(halo, k, 0)
    top = jnp.where(lax.broadcasted_iota(jnp.int32, halo.shape, 0) < k, p, r[:8])
    return jnp.concatenate([top, r[8:]], axis=0)


def _shift_up(cur, halo, k):
    if k == 0:
        return cur
    n = cur.shape[0]
    r = pltpu.roll(cur, n - k, 0)
    p = pltpu.roll(halo, 8 - k, 0)
    bot = jnp.where(lax.broadcasted_iota(jnp.int32, halo.shape, 0) >= 8 - k, p, r[n - 8:])
    return jnp.concatenate([r[: n - 8], bot], axis=0)


def _f_qkv_act(pre, jb, heads):
    act = _silu(pre)
    nrm = act * lax.rsqrt(jnp.sum(act * act, axis=-1, keepdims=True) + EPS)
    nrm = nrm * jnp.where(jb < heads, HEAD_DIM ** -0.5, 1.0)
    return jnp.where(jb < 2 * heads, nrm, act)


def _taps(w_ref):
    return [w_ref[j:j + 1, :] for j in range(CONV_TAPS)]


def _conv_pre(x_ref, w, r0, tr):
    cur = x_ref[pl.ds(r0, tr), :]
    halo = x_ref[pl.ds(jnp.maximum(r0 - 8, 0), 8), :]
    halo = jnp.where(r0 > 0, halo, 0.0)
    shifted = [_shift_down(cur, halo, CONV_TAPS - 1 - j) for j in range(CONV_TAPS)]
    pre = sum(w[j] * shifted[j] for j in range(CONV_TAPS))
    return pre, shifted


def _bprep_fwd(proj, col0, conv_w8, heads, tr):
    s_len = proj.shape[0]
    ncol = 3 * heads
    cb0 = col0 // LANE

    def body(x_ref, w_ref, o_ref):
        jb = pl.program_id(0)
        w = _taps(w_ref)

        def step(t, _):
            r0 = pl.multiple_of(t * tr, tr)
            pre, _unused = _conv_pre(x_ref, w, r0, tr)
            o_ref[pl.ds(r0, tr), :] = _f_qkv_act(pre, jb, heads)
            return 0

        lax.fori_loop(0, s_len // tr, step, 0)

    return _call(
        body,
        name="bprep_fwd",
        grid=(ncol,),
        in_specs=[
            pl.BlockSpec((s_len, LANE), lambda j: (0, cb0 + j)),
            pl.BlockSpec((8, LANE), lambda j: (0, j)),
        ],
        out_specs=pl.BlockSpec((s_len, LANE), lambda j: (0, j)),
        out_shape=jax.ShapeDtypeStruct((s_len, ncol * LANE), F32),
        compiler_params=_params("parallel"),
    )(proj, conv_w8)


def _bprep_bwd(proj, col0, conv_w8, dqkv, heads, tr):
    s_len = proj.shape[0]
    ncol = 3 * heads
    cb0 = col0 // LANE
    nt = s_len // tr

    def body(x_ref, w_ref, d_ref, dx_ref, dw_ref, dpre_ref):
        jb = pl.program_id(0)
        w = _taps(w_ref)

        def pass1(t, dws):
            r0 = pl.multiple_of(t * tr, tr)
            pre, shifted = _conv_pre(x_ref, w, r0, tr)
            _, vjp = jax.vjp(lambda p: _f_qkv_act(p, jb, heads), pre)
            (dpre,) = vjp(d_ref[pl.ds(r0, tr), :])
            dpre_ref[pl.ds(r0, tr), :] = dpre
            return tuple(dws[j] + jnp.sum(dpre * shifted[j], axis=0, keepdims=True) for j in range(CONV_TAPS))

        dws = lax.fori_loop(0, nt, pass1, tuple(jnp.zeros((1, LANE), F32) for _ in range(CONV_TAPS)))
        for j in range(CONV_TAPS):
            dw_ref[j:j + 1, :] = dws[j]
        dw_ref[CONV_TAPS:, :] = jnp.zeros((8 - CONV_TAPS, LANE), F32)

        def pass2(t, _):
            r0 = pl.multiple_of(t * tr, tr)
            cur = dpre_ref[pl.ds(r0, tr), :]
            halo = dpre_ref[pl.ds(jnp.minimum(r0 + tr, s_len - 8), 8), :]
            halo = jnp.where(r0 + tr < s_len, halo, 0.0)
            dx = sum(w[j] * _shift_up(cur, halo, CONV_TAPS - 1 - j) for j in range(CONV_TAPS))
            dx_ref[pl.ds(r0, tr), :] = dx.astype(BF16)
            return 0

        lax.fori_loop(0, nt, pass2, 0)

    col = pl.BlockSpec((s_len, LANE), lambda j: (0, j))
    w_spec = pl.BlockSpec((8, LANE), lambda j: (0, j))
    return _call(
        body,
        name="bprep_bwd",
        grid=(ncol,),
        in_specs=[pl.BlockSpec((s_len, LANE), lambda j: (0, cb0 + j)), w_spec, col],
        out_specs=[col, w_spec],
        out_shape=[jax.ShapeDtypeStruct((s_len, ncol * LANE), BF16), jax.ShapeDtypeStruct((8, ncol * LANE), F32)],
        scratch_shapes=[pltpu.VMEM((s_len, LANE), F32)],
        compiler_params=_params("parallel"),
    )(proj, conv_w8, dqkv)


def _f_gbeta(bt, at, a_log, dt_bias):
    xs = at + dt_bias
    softplus = jnp.maximum(xs, 0.0) + jnp.log(1.0 + jnp.exp(-jnp.abs(xs)))
    return jax.nn.sigmoid(bt), -jnp.exp(a_log) * softplus


def _small(name, fn, ins, out_shapes):
    def body(*refs):
        vals = fn(*[r[...] for r in refs[:len(ins)]])
        for ref, val in zip(refs[len(ins):], vals):
            ref[...] = val.astype(ref.dtype)

    return _call(body, name=name, out_shape=[jax.ShapeDtypeStruct(s, F32) for s in out_shapes],
                 compiler_params=_params())(*ins)


def _dot3(a, b, dims=(((1,), (0,)), ((), ()))):
    ah, bh = a.astype(BF16), b.astype(BF16)
    al, bl = (a - ah.astype(F32)).astype(BF16), (b - bh.astype(F32)).astype(BF16)
    d = lambda u, v: lax.dot_general(u, v, dims, preferred_element_type=F32)
    return d(ah, bh) + (d(ah, bl) + d(al, bh))


def _bdot(a, b, dims=(((1,), (0,)), ((), ()))):
    return lax.dot_general(a.astype(BF16), b.astype(BF16), dims, preferred_element_type=F32)


def _gdn_group(s0, q, k, v, g_row, b_row):
    gs = q.shape[0]
    r = lax.broadcasted_iota(jnp.int32, (gs, gs), 0)
    i = lax.broadcasted_iota(jnp.int32, (gs, gs), 1)
    same = (r // GDN_CHUNK) == (i // GDN_CHUNK)
    g_b = jnp.broadcast_to(g_row, (gs, gs))
    b_b = jnp.broadcast_to(b_row, (gs, gs))
    eye = r == i
    g_col = jnp.sum(jnp.where(eye, g_b, 0.0), axis=1, keepdims=True)
    b_col = jnp.sum(jnp.where(eye, b_b, 0.0), axis=1, keepdims=True)
    gc_col = jnp.sum(jnp.where(same & (i <= r), g_b, 0.0), axis=1, keepdims=True)
    gc_row = jnp.sum(jnp.where(same & (r <= i), jnp.broadcast_to(g_col, (gs, gs)), 0.0), axis=0, keepdims=True)
    gl_col = jnp.sum(jnp.where(same, g_b, 0.0), axis=1, keepdims=True)
    tril = same & (i <= r)
    decay = jnp.where(tril, jnp.exp(jnp.where(tril, gc_col - gc_row, 0.0)), 0.0)
    kk = _bdot(k, k, _NT)
    nmat = jnp.where(same & (i < r), b_col * kk * decay, 0.0)
    am = -nmat
    tinv = jnp.where(eye, 1.0, 0.0) + am
    steps = GDN_CHUNK.bit_length() - 1
    for _ in range(steps - 1):
        am = _dot3(am, am)
        tinv = tinv + _dot3(tinv, am)
    eg = jnp.exp(gc_col)
    w_v = _dot3(tinv, b_col * v)
    w_k = _dot3(tinv, (b_col * eg) * k)
    attn = _bdot(q, k, _NT) * decay
    q_g = q * eg
    k_dec = k * jnp.exp(gl_col - gc_col)
    ridx = lax.broadcasted_iota(jnp.int32, (gs, 1), 0)
    s = s0
    u_all = jnp.zeros_like(v)
    o_inter = jnp.zeros_like(v)
    for c in range(gs // GDN_CHUNK):
        in_chunk = (ridx // GDN_CHUNK) == c
        u = jnp.where(in_chunk, w_v - _bdot(w_k, s), 0.0)
        o_inter = o_inter + jnp.where(in_chunk, _bdot(q_g, s), 0.0)
        u_all = u_all + u
        gl = jnp.sum(jnp.where(ridx == c * GDN_CHUNK, gl_col, 0.0), axis=0, keepdims=True)
        s = jnp.exp(gl) * s + _bdot(k_dec, u, _TN)
    return o_inter + _bdot(attn, u_all), s


def _gdn_specs(heads, ngrp, rev):
    blk = (lambda n: ngrp - 1 - n) if rev else (lambda n: n)
    def col(off):
        return pl.BlockSpec((GDN_GROUP, LANE), lambda h, n: (blk(n), off * heads + h))
    vec = pl.BlockSpec((1, ngrp, GDN_GROUP), lambda h, n: (h, 0, 0))
    state = pl.BlockSpec((1, 1, HEAD_DIM, HEAD_DIM), lambda h, n: (h, blk(n), 0, 0))
    return col, vec, state, blk


def _gdn_fwd(qkv, g3, b3, heads):
    s_len = qkv.shape[0]
    ngrp = s_len // GDN_GROUP
    col, vec, state, _ = _gdn_specs(heads, ngrp, False)

    def body(q_ref, k_ref, v_ref, g_ref, b_ref, o_ref, st_ref, s_scr):
        n = pl.program_id(1)

        @pl.when(n == 0)
        def _():
            s_scr[...] = jnp.zeros_like(s_scr)

        s0 = s_scr[...]
        st_ref[0, 0] = s0
        o, s1 = _gdn_group(s0, q_ref[...], k_ref[...], v_ref[...], g_ref[0, pl.ds(n, 1), :], b_ref[0, pl.ds(n, 1), :])
        o_ref[...] = o
        s_scr[...] = s1

    return _call(
        body,
        name="gdn_fwd",
        grid=(heads, ngrp),
        in_specs=[col(0), col(1), col(2), vec, vec],
        out_specs=[pl.BlockSpec((GDN_GROUP, LANE), lambda h, n: (n, h)), state],
        out_shape=[
            jax.ShapeDtypeStruct((s_len, heads * LANE), F32),
            jax.ShapeDtypeStruct((heads, ngrp, HEAD_DIM, HEAD_DIM), F32),
        ],
        scratch_shapes=[pltpu.VMEM((HEAD_DIM, HEAD_DIM), F32)],
        compiler_params=_params("parallel", "arbitrary"),
    )(qkv, qkv, qkv, g3, b3)


def _gdn_bwd(qkv, g3, b3, states, do, heads):
    s_len = qkv.shape[0]
    ngrp = s_len // GDN_GROUP
    col, vec, state, blk = _gdn_specs(heads, ngrp, True)
    out_col = pl.BlockSpec((GDN_GROUP, LANE), lambda h, n: (blk(n), h))

    def body(q_ref, k_ref, v_ref, g_ref, b_ref, st_ref, do_ref, dq_ref, dk_ref, dv_ref, dg_ref, db_ref, ds_scr):
        n = pl.program_id(1)
        grp = ngrp - 1 - n

        @pl.when(n == 0)
        def _():
            ds_scr[...] = jnp.zeros_like(ds_scr)

        _, vjp = jax.vjp(_gdn_group, st_ref[0, 0], q_ref[...], k_ref[...], v_ref[...],
                         g_ref[0, pl.ds(grp, 1), :], b_ref[0, pl.ds(grp, 1), :])
        ds0, dq, dk, dv, dg, db = vjp((do_ref[...], ds_scr[...]))
        dq_ref[...] = dq
        dk_ref[...] = dk
        dv_ref[...] = dv
        dg_ref[0, pl.ds(grp, 1), :] = dg
        db_ref[0, pl.ds(grp, 1), :] = db
        ds_scr[...] = ds0

    shp = jax.ShapeDtypeStruct((s_len, heads * LANE), F32)
    vshp = jax.ShapeDtypeStruct(g3.shape, F32)
    return _call(
        body,
        name="gdn_bwd",
        grid=(heads, ngrp),
        in_specs=[col(0), col(1), col(2), vec, vec, state, out_col],
        out_specs=[out_col, out_col, out_col, vec, vec],
        out_shape=[shp, shp, shp, vshp, vshp],
        scratch_shapes=[pltpu.VMEM((HEAD_DIM, HEAD_DIM), F32)],
        compiler_params=_params("parallel", "arbitrary"),
    )(qkv, qkv, qkv, g3, b3, states, do)


def _sum_adam(name, parts, w, m, v, tr):
    n, rows, cols = parts.shape

    def body(p_ref, w_ref, m_ref, v_ref, g_out, d_out, m_out, v_out):
        g = p_ref[0].astype(F32)
        for s in range(1, n):
            g = g + p_ref[s].astype(F32)
        m_new = ADAM_B1 * m_ref[...] + (1.0 - ADAM_B1) * g
        v_new = ADAM_B2 * v_ref[...] + (1.0 - ADAM_B2) * (g * g)
        m_hat = m_new / (1.0 - ADAM_B1 ** ADAM_STEP)
        v_hat = v_new / (1.0 - ADAM_B2 ** ADAM_STEP)
        g_out[...] = g
        d_out[...] = -ADAM_LR * (m_hat / (jnp.sqrt(v_hat) + ADAM_EPS) + ADAM_WD * w_ref[...])
        m_out[...] = m_new
        v_out[...] = v_new

    mat = pl.BlockSpec((tr, cols), lambda i: (i, 0))
    shp = jax.ShapeDtypeStruct((rows, cols), F32)
    return _call(
        body,
        name=name,
        grid=(rows // tr,),
        in_specs=[pl.BlockSpec((n, tr, cols), lambda i: (0, i, 0)), mat, mat, mat],
        out_specs=[mat, mat, mat, mat],
        out_shape=[shp, shp, shp, shp],
        compiler_params=_params("parallel"),
    )(parts, w, m, v)


def _sum_parts(name, parts):
    n = parts.shape[0]

    def fn(p):
        g = p[0]
        for s in range(1, n):
            g = g + p[s]
        return (g,)

    return _small(name, fn, [parts], [parts.shape[1:]])[0]


def _pad_rows(a, mult):
    r = (-a.shape[0]) % mult
    return a if r == 0 else jnp.concatenate([a, jnp.zeros((r,) + a.shape[1:], a.dtype)], axis=0)


def _pad_lanes(a, width):
    return jnp.concatenate([a, jnp.zeros(a.shape[:-1] + (width - a.shape[-1],), a.dtype)], axis=-1)


def _pack(pieces, width, mult):
    return _pad_rows(jnp.concatenate([p.reshape(-1, width) for p in pieces], axis=0), mult)


def _unpack(packed, shapes, width):
    out, r0 = [], 0
    for shp in shapes:
        size = 1
        for d in shp:
            size *= d
        nr = size // width
        out.append(packed[..., r0:r0 + nr, :].reshape(packed.shape[:-2] + tuple(shp)))
        r0 += nr
    return out


def kernel(x, c, w_mod, b_mod, norm1_w, w_in, q_norm_w, k_norm_w, conv_w, a_log, dt_bias, o_norm_w, p_a, p_b, w_out, norm2_w, w_gate, w_up, w_down, loss_target, m_w_mod, m_b_mod, m_norm1_w, m_w_in, m_q_norm_w, m_k_norm_w, m_conv_w, m_a_log, m_dt_bias, m_o_norm_w, m_p_a, m_p_b, m_w_out, m_norm2_w, m_w_gate, m_w_up, m_w_down, v_w_mod, v_b_mod, v_norm1_w, v_w_in, v_q_norm_w, v_k_norm_w, v_conv_w, v_a_log, v_dt_bias, v_o_norm_w, v_p_a, v_p_b, v_w_out, v_norm2_w, v_w_gate, v_w_up, v_w_down):
    s_len, d = x.shape[1], x.shape[2]
    heads = a_log.shape[1]
    dh = heads * HEAD_DIM
    f = w_down.shape[1] * N_DEV
    din_loc = w_in.shape[2]
    me = _my_id()
    x2, tgt = x[0], loss_target[0]

    big = [w_in, p_a, p_b, w_out, w_gate, w_up, w_down]
    big_shapes = [t.shape[1:] for t in big]
    wg = _exchange("gather_weights", _pack([t[0].astype(BF16) for t in big], d, LANE), False)
    g_in, g_pa, g_pb, g_out, g_gate, g_up, g_down = _unpack(wg, big_shapes, d)
    cols = lambda t: t.transpose(1, 0, 2).reshape(t.shape[1], -1)
    rows = lambda t: t.reshape(-1, t.shape[2])
    w_in_g = cols(g_in)
    o_ba = 3 * dh + 3 * dh + dh
    din = w_in_g.shape[1]
    n_perm = o_ba + 2 * d + LANE
    w_in_p = jnp.concatenate(
        [w_in_g[:, :o_ba], w_in_g[:, o_ba + 2 * heads:], w_in_g[:, o_ba:o_ba + 2 * heads],
         jnp.zeros((d, LANE - 2 * heads), BF16)], axis=1)
    o_qkvb, o_z, o_ga, o_bad = 3 * dh, 6 * dh, 7 * dh, 7 * dh + 2 * d
    w_pa, w_pb, w_o, w_dn = rows(g_pa), rows(g_pb), rows(g_out), rows(g_down)
    w_gu = jnp.concatenate([cols(g_gate), cols(g_up)], axis=1)

    c_all = _pad_rows(_exchange("gather_c", c, False).reshape(N_DEV, d), LANE)
    mod_part = _mm("mod_fwd", c_all, w_mod[0], a_fn=_silu, tk=d)[:N_DEV]
    mod_all = _exchange("gather_mod", mod_part, False)
    mod_me = lax.dynamic_index_in_dim(mod_all, me, axis=1, keepdims=False).reshape(1, 6 * d) + b_mod
    sh1, sc1, gt1, sh2, sc2, gt2 = [mod_me[:, j * d:(j + 1) * d] for j in range(6)]

    tm = min(128, s_len)
    th = min(1024, s_len)
    (u1,) = _seg("pre1_fwd", lambda a, nw, sc, sh: (_f_mod(a, nw, sc, sh),),
                 [_row(x2), _par(norm1_w), _par(sc1), _par(sh1)], [(d, BF16, "row", d, 0, 0)], s_len, tm)
    proj = _mm("proj", u1, w_in_p, tm=1024, tn=640, tk=d)

    hb = dh // LANE
    qn, kn, vb = _seg(
        "qk_fwd", lambda q, k, v, qw, kw: _f_qk(q, k, qw, kw) + (v,),
        [_row(proj, LANE, 0, 1), _row(proj, LANE, hb, 1), _row(proj, LANE, 2 * hb, 1), _par(q_norm_w), _par(k_norm_w)],
        [(dh, BF16, "row", LANE, 0, 1)] * 3, s_len, th, nh=hb)
    bq = min(512, s_len)
    o_a = _sb_fwd(qn, kn, vb, heads, bq)

    conv_all = _exchange("gather_conv", conv_w[0], False)
    conv_g = _pad_rows(conv_all.transpose(1, 0, 2).reshape(CONV_TAPS, 3 * dh), 8)
    tr = min(512, s_len)
    qkv_b = _bprep_fwd(proj, o_qkvb, conv_g, heads, tr)
    ba_t = proj[:, o_bad:o_bad + 2 * heads].T
    a_col, dt_col = a_log.reshape(heads, 1), dt_bias.reshape(heads, 1)
    beta_t, g_t = _small("gbeta_fwd", _f_gbeta, [ba_t[:heads], ba_t[heads:], a_col, dt_col], [(heads, s_len)] * 2)
    ngrp = s_len // GDN_GROUP
    g3, b3 = g_t.reshape(heads, ngrp, GDN_GROUP), beta_t.reshape(heads, ngrp, GDN_GROUP)
    o_b, states = _gdn_fwd(qkv_b, g3, b3, heads)
    zoff = o_z // LANE
    (ob2,) = _seg("post_fwd", lambda ob, z, ow: (_f_post(ob, z, ow),),
                  [_row(o_b, LANE, 0, 1), _row(proj, LANE, zoff, 1), _par(o_norm_w)],
                  [(dh, BF16, "row", LANE, 0, 1)], s_len, th, nh=hb)

    ya = _mm("ya", o_a, w_pa, tm=1024, tk=dh)
    yb = _mm("yb", ob2, w_pb, tm=1024, tk=dh)
    goff = o_ga // d
    gate_ins = [_row(proj, d, goff, 0), _row(proj, d, goff + 1, 0)]
    (merged,) = _seg("merge_fwd", lambda a, b, ga, gb: (_f_merge(a, b, ga, gb),),
                     [_row(ya), _row(yb)] + gate_ins, [(d, BF16, "row", d, 0, 0)], s_len, tm)
    t_out = _mm("attn_out", merged, w_o, tm=1024, tk=d)
    res_par = [_par(gt1), _par(norm2_w), _par(sc2), _par(sh2)]
    h1, u2 = _seg("res1_fwd", _f_res, [_row(x2), _row(t_out)] + res_par,
                  [(d, F32, "row", d, 0, 0), (d, BF16, "row", d, 0, 0)], s_len, tm)
    gu = _mm("ffn_in", u2, w_gu, tm=1024, tk=d)
    tf = 128
    (ff,) = _seg("ff_fwd", lambda a, b: (_f_ff(a, b),), [_row(gu, f, 0, 0), _row(gu, f, 1, 0)],
                 [(f, BF16, "row", f, 0, 0)], s_len, tf)
    dn = _mm("ffn_out", ff, w_dn, tm=1024, tk=512)

    def f_loss(hh, dd, g2, tg):
        err = hh + g2 * dd - tg
        dh2 = err * (1.0 / d)
        return (dh2, g2 * dh2, jnp.sum(err * err, axis=0, keepdims=True), jnp.sum(dh2 * dd, axis=0, keepdims=True))

    dh2, ddn, lsq, dgt2 = _seg(
        "loss", f_loss, [_row(h1), _row(dn), _par(gt2), _row(tgt)],
        [(d, F32, "row", d, 0, 0), (d, BF16, "row", d, 0, 0), (d, F32, "acc", d, 0, 0), (d, F32, "acc", d, 0, 0)],
        s_len, tm)
    loss = lax.psum(0.5 * jnp.sum(lsq) / d, MESH_AXES)

    dff = _mm("d_ff", ddn, w_dn, tb=True, tm=1024, tk=d)
    dw_down = _mm("dw_down", ff, ddn, ta=True, out_dtype=BF16, tk=1024)

    def f_ff_bwd(a, b, dy):
        _, vjp = jax.vjp(_f_ff, a, b)
        da, db = vjp(dy)
        return (jnp.concatenate([da, db], axis=1),)

    (dgu,) = _seg("ff_bwd", f_ff_bwd, [_row(gu, f, 0, 0), _row(gu, f, 1, 0), _row(dff)],
                  [(2 * f, BF16, "row", 2 * f, 0, 0)], s_len, tf)
    du2 = _mm("d_u2", dgu, w_gu, tb=True, tm=1024, tk=1024)
    dw_gu = _mm("dw_gu", u2, dgu, ta=True, out_dtype=BF16, tk=1024)

    def f_res_bwd(a, t, g, nw, sc, sh, dhd, du):
        _, vjp = jax.vjp(_f_res, a, t, g, nw, sc, sh)
        da, dt_, dg, dnw, dsc, dsh = vjp((dhd, du))
        return da, dt_, dg, dnw, dsc, dsh

    acc_d = (d, F32, "acc", d, 0, 0)
    dh1, dt_out, dgt1, dnorm2, dsc2, dsh2 = _seg(
        "res1_bwd", f_res_bwd, [_row(x2), _row(t_out)] + res_par + [_row(dh2), _row(du2)],
        [(d, F32, "row", d, 0, 0), (d, BF16, "row", d, 0, 0), acc_d, acc_d, acc_d, acc_d], s_len, tm)

    dmerged = _mm("d_merged", dt_out, w_o, tb=True, tm=1024, tk=d)
    dw_out = _mm("dw_out", merged, dt_out, ta=True, out_dtype=BF16, tk=1024)

    def f_merge_bwd(a, b, ga, gb, dy):
        _, vjp = jax.vjp(_f_merge, a, b, ga, gb)
        da, db, dga, dgb = vjp(dy)
        return da, db, jnp.concatenate([dga, dgb], axis=1)

    dya, dyb, dgates = _seg("merge_bwd", f_merge_bwd, [_row(ya), _row(yb)] + gate_ins + [_row(dmerged)],
                            [(d, BF16, "row", d, 0, 0), (d, BF16, "row", d, 0, 0), (2 * d, BF16, "row", 2 * d, 0, 0)],
                            s_len, tm)
    do_a = _mm("d_oa", dya, w_pa, tb=True, tm=1024, tk=d)
    dw_pa = _mm("dw_pa", o_a, dya, ta=True, out_dtype=BF16, tk=1024)
    dob2 = _mm("d_ob2", dyb, w_pb, tb=True, tm=1024, tk=d)
    dw_pb = _mm("dw_pb", ob2, dyb, ta=True, out_dtype=BF16, tk=1024)

    def f_post_bwd(ob, z, ow, dy):
        _, vjp = jax.vjp(_f_post, ob, z, ow)
        return vjp(dy)

    acc_h = (LANE, F32, "acc", LANE, 0, 0)
    d_ob, dz, d_onorm = _seg(
        "post_bwd", f_post_bwd,
        [_row(o_b, LANE, 0, 1), _row(proj, LANE, zoff, 1), _par(o_norm_w), _row(dob2, LANE, 0, 1)],
        [(dh, F32, "row", LANE, 0, 1), (dh, BF16, "row", LANE, 0, 1), acc_h], s_len, th, nh=hb)
    dqb, dkb, dvb, dg3, db3 = _gdn_bwd(qkv_b, g3, b3, states, d_ob, heads)
    dqkv_n = jnp.concatenate([dqb, dkb, dvb], axis=1)
    dqkv_pre, dconv8 = _bprep_bwd(proj, o_qkvb, conv_g, dqkv_n, heads, tr)

    def f_gbeta_bwd(bt, at, al, dtb, dbeta, dg):
        _, vjp = jax.vjp(_f_gbeta, bt, at, al, dtb)
        return vjp((dbeta, dg))

    dbt, dat, da_log, ddt = _small(
        "gbeta_bwd", f_gbeta_bwd,
        [ba_t[:heads], ba_t[heads:], a_col, dt_col, db3.reshape(heads, s_len), dg3.reshape(heads, s_len)],
        [(heads, s_len), (heads, s_len), (heads, 1), (heads, 1)])
    dba = _pad_lanes(jnp.concatenate([dbt, dat], axis=0).T, LANE).astype(BF16)

    dqn, dkn, dva = _sb_bwd(qn, kn, vb, o_a, do_a, heads, bq)

    def f_qk_bwd(q, k, qw, kw, dq, dk):
        _, vjp = jax.vjp(_f_qk, q, k, qw, kw)
        return vjp((dq, dk))

    dqa, dka, d_qnorm, d_knorm = _seg(
        "qk_bwd", f_qk_bwd,
        [_row(proj, LANE, 0, 1), _row(proj, LANE, hb, 1), _par(q_norm_w), _par(k_norm_w),
         _row(dqn, LANE, 0, 1), _row(dkn, LANE, 0, 1)],
        [(dh, BF16, "row", LANE, 0, 1), (dh, BF16, "row", LANE, 0, 1), acc_h, acc_h], s_len, th, nh=hb)

    dproj = jnp.concatenate([dqa, dka, dva.astype(BF16), dqkv_pre, dz, dgates, dba], axis=1)
    du1 = _mm("d_u1", dproj, w_in_p, tb=True, tm=1024, tk=640)
    dw_in_p = _mm("dw_in", u1, dproj, ta=True, out_dtype=BF16, tn=640, tk=1024)

    def f_pre_bwd(a, nw, sc, sh, du, dres):
        _, vjp = jax.vjp(_f_mod, a, nw, sc, sh)
        da, dnw, dsc, dsh = vjp(du)
        return da + dres, dnw, dsc, dsh

    dx, dnorm1, dsc1, dsh1 = _seg(
        "pre1_bwd", f_pre_bwd, [_row(x2), _par(norm1_w), _par(sc1), _par(sh1), _row(du1), _row(dh1)],
        [(d, F32, "row", d, 0, 0), acc_d, acc_d, acc_d], s_len, tm)

    dmod = jnp.concatenate([dsh1, dsc1, dgt1, dsh2, dsc2, dgt2], axis=1)
    lanes = lambda t: _pad_lanes(t.reshape(1, -1), LANE)
    small = [dmod, dnorm1, d_qnorm, d_knorm, lanes(da_log), lanes(ddt), d_onorm, dnorm2, dconv8[:CONV_TAPS]]
    small_shapes = [t.shape for t in small]
    small_all = _exchange("gather_small", _pack(small, LANE, 8), False)
    small_sum = _sum_parts("sum_small", small_all)
    g_bmod, g_n1, g_qn, g_kn, g_al, g_dt, g_on, g_n2, g_conv = _unpack(small_sum, small_shapes, LANE)
    ncv = conv_w.shape[2]
    g_conv_me = lax.dynamic_slice_in_dim(g_conv, me * ncv, ncv, axis=1)

    nmod = w_mod.shape[2]
    dmod_all = _unpack(small_all, small_shapes[:1], LANE)[0].reshape(N_DEV, 6 * d)
    dmod_cols = _pad_rows(lax.dynamic_slice_in_dim(dmod_all, me * nmod, nmod, axis=1), LANE)
    g_wmod = _mm("dw_mod", c_all, dmod_cols, ta=True, a_fn=_silu, tk=LANE)

    dw_in_g = jnp.concatenate([dw_in_p[:, :o_ba], dw_in_p[:, o_bad:o_bad + 2 * heads], dw_in_p[:, o_ga:o_ga + 2 * d]], axis=1)
    to_cols = lambda t: t.reshape(t.shape[0], N_DEV, -1).transpose(1, 0, 2)
    to_rows = lambda t: t.reshape(N_DEV, -1, t.shape[1])
    parts = [to_cols(dw_in_g), to_rows(dw_pa), to_rows(dw_pb), to_rows(dw_out),
             to_cols(dw_gu[:, :f]), to_cols(dw_gu[:, f:]), to_rows(dw_down)]
    send = jnp.concatenate([p.reshape(N_DEV, -1, d) for p in parts], axis=1)
    pad = (-send.shape[1]) % LANE
    if pad:
        send = jnp.concatenate([send, jnp.zeros((N_DEV, pad, d), BF16)], axis=1)
    recv = _exchange("scatter_grads", send, True)

    def adam_packed(name, parts_, ws, ms, vs, width, mult, tr_):
        outs = _sum_adam(name, parts_, _pack(ws, width, mult), _pack(ms, width, mult), _pack(vs, width, mult), tr_)
        shapes = [t.shape for t in ws]
        return [_unpack(o, shapes, width) for o in outs]

    big_m = [m_w_in, m_p_a, m_p_b, m_w_out, m_w_gate, m_w_up, m_w_down]
    big_v = [v_w_in, v_p_a, v_p_b, v_w_out, v_w_gate, v_w_up, v_w_down]
    gb, db_, mb, vb_ = adam_packed("adam_big", recv, big, big_m, big_v, d, LANE, LANE)

    gm, dm_, mm_, vm_ = adam_packed("adam_mod", g_wmod[None], [w_mod], [m_w_mod], [v_w_mod], nmod, 8, _tile(d, LANE))

    sm_w = [b_mod, norm1_w, q_norm_w, k_norm_w, lanes(a_log), lanes(dt_bias), o_norm_w, norm2_w, conv_w[0]]
    sm_m = [m_b_mod, m_norm1_w, m_q_norm_w, m_k_norm_w, lanes(m_a_log), lanes(m_dt_bias), m_o_norm_w, m_norm2_w, m_conv_w[0]]
    sm_v = [v_b_mod, v_norm1_w, v_q_norm_w, v_k_norm_w, lanes(v_a_log), lanes(v_dt_bias), v_o_norm_w, v_norm2_w, v_conv_w[0]]
    sm_g = [g_bmod, g_n1, g_qn, g_kn, g_al, g_dt, g_on, g_n2, g_conv_me]
    g_pack = _pack(sm_g, LANE, 8)
    gs_, ds_, ms_, vs_ = adam_packed("adam_small", g_pack[None], sm_w, sm_m, sm_v, LANE, 8, g_pack.shape[0])

    def assemble(big_l, mod_l, small_l):
        s_bmod, s_n1, s_qn, s_kn, s_al, s_dt, s_on, s_n2, s_conv = small_l
        b_in, b_pa, b_pb, b_out, b_gate, b_up, b_down = big_l
        return [mod_l[0], s_bmod, s_n1, b_in, s_qn, s_kn, s_conv[None], s_al[:, :heads], s_dt[:, :heads], s_on,
                b_pa, b_pb, b_out, s_n2, b_gate, b_up, b_down]

    outs = [loss, dx[None]]
    for big_l, mod_l, small_l in ((gb, gm, gs_), (db_, dm_, ds_), (mb, mm_, ms_), (vb_, vm_, vs_)):
        outs += assemble(big_l, mod_l, small_l)
    return tuple(outs)
```

```python
import jax
import jax.numpy as jnp
from jax import lax
from jax.experimental import pallas as pl
from jax.experimental.pallas import tpu as pltpu

F32, BF16 = jnp.float32, jnp.bfloat16
EPS = 1e-6
HEAD_DIM = 128
GDN_CHUNK = 64
GDN_GROUP = 256
CONV_TAPS = 4
N_DEV = 8
MESH_AXES = ("x", "y", "c")
ADAM_LR, ADAM_B1, ADAM_B2, ADAM_EPS, ADAM_WD, ADAM_STEP = 0.001, 0.9, 0.999, 1e-08, 0.01, 10
VMEM_LIMIT_BYTES = 56 * 1024 * 1024
LANE = 128
MESH_ID = pl.DeviceIdType.MESH


def _call(body, **kw):
    return pl.pallas_call(body, **kw)


def _params(*sem):
    return pltpu.CompilerParams(dimension_semantics=sem or None, vmem_limit_bytes=VMEM_LIMIT_BYTES)


def _tile(n, target):
    t = (min(n, target) // LANE) * LANE
    while t >= LANE:
        if n % t == 0:
            return t
        t -= LANE
    return n


def _silu(x):
    return x * jax.nn.sigmoid(x)


def _my_id():
    return 4 * lax.axis_index("x") + 2 * lax.axis_index("y") + lax.axis_index("c")


def _exchange(name, src, scatter):
    blk = src.shape[1:] if scatter else src.shape

    def body(src_ref, out_ref, send_sems, recv_sems, local_sem):
        x, y, c = lax.axis_index("x"), lax.axis_index("y"), lax.axis_index("c")
        me = 4 * x + 2 * y + c

        def peer(k):
            kx, ky, kc = (k >> 2) & 1, (k >> 1) & 1, k & 1
            px, py, pc = x ^ kx, y ^ ky, c ^ kc
            return (px, py, pc), 4 * px + 2 * py + pc

        def copy(k):
            dev, pid = peer(k)
            return pltpu.make_async_remote_copy(
                src_ref=src_ref.at[pid] if scatter else src_ref,
                dst_ref=out_ref.at[me],
                send_sem=send_sems.at[k - 1],
                recv_sem=recv_sems.at[k - 1],
                device_id=dev,
                device_id_type=MESH_ID,
            )

        def arrival(k):
            dev, pid = peer(k)
            return pltpu.make_async_remote_copy(
                src_ref=src_ref.at[pid] if scatter else src_ref,
                dst_ref=out_ref.at[pid],
                send_sem=send_sems.at[k - 1],
                recv_sem=recv_sems.at[k - 1],
                device_id=dev,
                device_id_type=MESH_ID,
            )

        mine = pltpu.make_async_copy(src_ref.at[me] if scatter else src_ref, out_ref.at[me], local_sem)
        mine.start()
        for k in range(1, N_DEV):
            copy(k).start()
        for k in range(1, N_DEV):
            arrival(k).wait_recv()
        for k in range(1, N_DEV):
            copy(k).wait_send()
        mine.wait()

    return _call(
        body,
        name=name,
        out_shape=jax.ShapeDtypeStruct((N_DEV,) + tuple(blk), src.dtype),
        in_specs=[pl.BlockSpec(memory_space=pl.ANY)],
        out_specs=pl.BlockSpec(memory_space=pl.ANY),
        scratch_shapes=[
            pltpu.SemaphoreType.DMA((N_DEV - 1,)),
            pltpu.SemaphoreType.DMA((N_DEV - 1,)),
            pltpu.SemaphoreType.DMA,
        ],
    )(src)


def _gather_two_level(name, src):
    def body(src_ref, out_ref, send_sems, recv_sems, local_sem):
        x, y, c = lax.axis_index("x"), lax.axis_index("y"), lax.axis_index("c")
        me, sibling = (x, y, c), (x, y, 1 - c)
        chips = [(1 - x, y), (x, 1 - y), (1 - x, 1 - y)]

        def slot(px, py, pc):
            return out_ref.at[4 * px + 2 * py + pc]

        def copy(k, block, to, from_src=False):
            return pltpu.make_async_remote_copy(
                src_ref=src_ref if from_src else slot(*block),
                dst_ref=slot(*block),
                send_sem=send_sems.at[k],
                recv_sem=recv_sems.at[k],
                device_id=to,
                device_id_type=MESH_ID,
            )

        mine = pltpu.make_async_copy(src_ref, slot(*me), local_sem)
        mine.start()
        first = [copy(0, me, sibling, True)] + [copy(1 + j, me, (*chip, c), True) for j, chip in enumerate(chips)]
        for cp in first:
            cp.start()
        passed = [copy(4 + j, (*chip, c), sibling) for j, chip in enumerate(chips)]
        for j, chip in enumerate(chips):
            copy(1 + j, (*chip, c), me).wait_recv()
            passed[j].start()
        copy(0, sibling, me).wait_recv()
        for j, chip in enumerate(chips):
            copy(4 + j, (*chip, 1 - c), me).wait_recv()
        for cp in first + passed:
            cp.wait_send()
        mine.wait()

    return _call(
        body,
        name=name,
        out_shape=jax.ShapeDtypeStruct((N_DEV,) + tuple(src.shape), src.dtype),
        in_specs=[pl.BlockSpec(memory_space=pl.ANY)],
        out_specs=pl.BlockSpec(memory_space=pl.ANY),
        scratch_shapes=[
            pltpu.SemaphoreType.DMA((N_DEV - 1,)),
            pltpu.SemaphoreType.DMA((N_DEV - 1,)),
            pltpu.SemaphoreType.DMA,
        ],
    )(src)


def _mm(name, a, b, *, ta=False, tb=False, out_dtype=F32, a_fn=None, tm=512, tn=512, tk=512):
    m, kdim = (a.shape[1], a.shape[0]) if ta else a.shape
    n = b.shape[0] if tb else b.shape[1]
    assert kdim == (b.shape[1] if tb else b.shape[0]), (a.shape, b.shape, ta, tb)
    tm, tn, tk = _tile(m, tm), _tile(n, tn), _tile(kdim, tk)
    nk = kdim // tk
    a_spec = pl.BlockSpec((tk, tm), lambda i, j, k: (k, i)) if ta else pl.BlockSpec((tm, tk), lambda i, j, k: (i, k))
    b_spec = pl.BlockSpec((tn, tk), lambda i, j, k: (j, k)) if tb else pl.BlockSpec((tk, tn), lambda i, j, k: (k, j))
    dims = (((0 if ta else 1,), (1 if tb else 0,)), ((), ()))

    def body(a_ref, b_ref, o_ref, acc_ref):
        k = pl.program_id(2)
        av = a_ref[...]
        if a_fn is not None:
            av = a_fn(av.astype(F32))
        p = lax.dot_general(av.astype(BF16), b_ref[...].astype(BF16), dims, preferred_element_type=F32)
        if nk == 1:
            o_ref[...] = p.astype(out_dtype)
        else:
            @pl.when(k == 0)
            def _():
                acc_ref[...] = p

            @pl.when(k > 0)
            def _():
                acc_ref[...] += p

            @pl.when(k == nk - 1)
            def _():
                o_ref[...] = acc_ref[...].astype(out_dtype)

    return _call(
        body,
        name=name,
        grid=(m // tm, n // tn, nk),
        in_specs=[a_spec, b_spec],
        out_specs=pl.BlockSpec((tm, tn), lambda i, j, k: (i, j)),
        out_shape=jax.ShapeDtypeStruct((m, n), out_dtype),
        scratch_shapes=[pltpu.VMEM((tm, tn) if nk > 1 else (8, LANE), F32)],
        compiler_params=_params("parallel", "parallel", "arbitrary"),
    )(a, b)


def _row(arr, w=None, off=0, stride=0):
    return (arr, "row", arr.shape[1] if w is None else w, off, stride)


def _par(arr, w=None, off=0, stride=0):
    return (arr, "par", arr.shape[1] if w is None else w, off, stride)


def _seg(name, fn, ins, outs, rows, tm, nh=1):
    nrow = rows // tm

    def spec(kind, w, off, stride):
        if kind == "row":
            return pl.BlockSpec((tm, w), lambda h, i: (i, off + stride * h))
        return pl.BlockSpec((1, w), lambda h, i: (0, off + stride * h))

    in_specs = [spec(kind, w, off, stride) for (_, kind, w, off, stride) in ins]
    out_specs = [spec("row" if kind == "row" else "par", w, off, stride) for (_, _, kind, w, off, stride) in outs]
    out_shape = [
        jax.ShapeDtypeStruct((rows if kind == "row" else 1, ncols), dt) for (ncols, dt, kind, _, _, _) in outs
    ]
    n_in = len(ins)

    def body(*refs):
        h, i = pl.program_id(0), pl.program_id(1)
        vals = fn(*[r[...] for r in refs[:n_in]])
        for (_, dt, kind, _, _, stride), ref, val in zip(outs, refs[n_in:], vals):
            if kind == "row":
                ref[...] = val.astype(dt)
            else:
                first = (i == 0) if stride != 0 else jnp.logical_and(i == 0, h == 0)

                @pl.when(first)
                def _(ref=ref, val=val):
                    ref[...] = val.astype(F32)

                @pl.when(jnp.logical_not(first))
                def _(ref=ref, val=val):
                    ref[...] += val.astype(F32)

    res = _call(
        body,
        name=name,
        grid=(nh, nrow),
        in_specs=in_specs,
        out_specs=out_specs,
        out_shape=out_shape,
        compiler_params=_params("arbitrary", "arbitrary"),
    )(*[t[0] for t in ins])
    return res


def _rmsn(x, w):
    return x * lax.rsqrt(jnp.mean(x * x, axis=-1, keepdims=True) + EPS) * w


def _f_mod(x, nw, sc, sh):
    return _rmsn(x, nw) * (1.0 + sc) + sh


def _f_res(x, t, g, nw, sc, sh):
    h = x + g * t
    return h, _f_mod(h, nw, sc, sh)


def _f_ff(gg, uu):
    return _silu(gg) * uu


def _f_merge(ya, yb, ga, gb):
    return jax.nn.sigmoid(ga) * ya + jax.nn.sigmoid(gb) * yb


def _f_qk(q, k, qw, kw):
    return _rmsn(q, qw), _rmsn(k, kw)


def _f_post(ob, z, ow):
    return _rmsn(ob, ow) * _silu(z)


def _f32(*xs):
    return [x.astype(F32) for x in xs]


def _log_sigmoid(z):
    return jnp.minimum(z, 0.0) - jnp.log(1.0 + jnp.exp(-jnp.abs(z)))


def _dot2(x, tri):
    hi = x.astype(BF16)
    lo = (x - hi.astype(F32)).astype(BF16)
    return jnp.dot(hi, tri, preferred_element_type=F32) + jnp.dot(lo, tri, preferred_element_type=F32)


def _tri(strict):
    j = lax.broadcasted_iota(jnp.int32, (LANE, LANE), 0)
    s = lax.broadcasted_iota(jnp.int32, (LANE, LANE), 1)
    return ((j > s) if strict else (j >= s)).astype(BF16)


SB_SLAB = 512
SB_UNROLL = 4


def _sb_slabs(bq, slab):
    return [slice(r * slab, (r + 1) * slab) for r in range(bq // slab)]


def _sb_diagonal(jj, n_slabs, slab):
    out = []
    for r in range(n_slabs):
        if jj * LANE >= (r + 1) * slab:
            continue
        out.append((r, "full" if (jj + 1) * LANE <= r * slab else "mask"))
    return out


def _sb_mask(r, jj, slab):
    t = r * slab + lax.broadcasted_iota(jnp.int32, (slab, LANE), 0)
    s = jj * LANE + lax.broadcasted_iota(jnp.int32, (slab, LANE), 1)
    return s < t


_NT = (((1,), (1,)), ((), ()))
_TN = (((0,), (0,)), ((), ()))


def _sb_fwd(qn, kn, vb, heads, bq):
    s_len = qn.shape[0]
    scale = HEAD_DIM ** -0.5
    sub = bq // LANE
    slab = min(SB_SLAB, bq)

    def body(q_ref, k_ref, v_ref, o_ref):
        i = pl.program_id(1)
        tri = _tri(True)
        slabs = _sb_slabs(bq, slab)
        qs = [q_ref[rw, :] for rw in slabs]

        def tile(q, k, v, cl, acc, mask):
            z = lax.dot_general(q, k, _NT, preferred_element_type=F32) * scale
            ls = _log_sigmoid(z)
            lm = ls - z
            if mask is not None:
                lm = jnp.where(mask, lm, 0.0)
            w = jnp.exp(ls + (_dot2(lm, tri) + cl))
            if mask is not None:
                w = jnp.where(mask, w, 0.0)
            return cl + jnp.sum(lm, axis=1, keepdims=True), acc + _dot2(w, v)

        state = [(jnp.zeros((slab, 1), F32), jnp.zeros((slab, LANE), F32)) for _ in slabs]
        for jj in reversed(range(sub)):
            j0 = pl.multiple_of(i * bq + jj * LANE, LANE)
            k, v = k_ref[pl.ds(j0, LANE), :], v_ref[pl.ds(j0, LANE), :]
            for r, kind in _sb_diagonal(jj, len(slabs), slab):
                state[r] = tile(qs[r], k, v, *state[r], _sb_mask(r, jj, slab) if kind == "mask" else None)

        def step(t, carry):
            for u in range(SB_UNROLL):
                j0 = pl.multiple_of((i * sub - 1 - (t * SB_UNROLL + u)) * LANE, LANE)
                k, v = k_ref[pl.ds(j0, LANE), :], v_ref[pl.ds(j0, LANE), :]
                carry = tuple(tile(qs[r], k, v, *carry[r], None) for r in range(len(slabs)))
            return carry

        state = lax.fori_loop(0, (i * sub) // SB_UNROLL, step, tuple(state))
        for r, rw in enumerate(slabs):
            o_ref[rw, :] = state[r][1]

    return _call(
        body,
        name="sb_fwd",
        grid=(heads, s_len // bq),
        in_specs=[
            pl.BlockSpec((bq, LANE), lambda h, i: (i, h)),
            pl.BlockSpec((s_len, LANE), lambda h, i: (0, h)),
            pl.BlockSpec((s_len, LANE), lambda h, i: (0, h)),
        ],
        out_specs=pl.BlockSpec((bq, LANE), lambda h, i: (i, h)),
        out_shape=jax.ShapeDtypeStruct(qn.shape, F32),
        compiler_params=_params("parallel", "arbitrary"),
    )(qn, kn, vb)


def _sb_bwd(qn, kn, vb, o, do, heads, bq):
    s_len = qn.shape[0]
    scale = HEAD_DIM ** -0.5
    sub = bq // LANE
    slab = min(SB_SLAB, bq)

    def body(q_ref, k_ref, v_ref, o_ref, do_ref, dq_ref, dk_ref, dv_ref):
        i = pl.program_id(1)

        @pl.when(i == 0)
        def _():
            dk_ref[...] = jnp.zeros_like(dk_ref)
            dv_ref[...] = jnp.zeros_like(dv_ref)

        tri, tri_inc = _tri(True), _tri(False)
        slabs = _sb_slabs(bq, slab)
        qs = [q_ref[rw, :] for rw in slabs]
        dobs = [do_ref[rw, :].astype(BF16) for rw in slabs]
        etots = [jnp.sum(dobs[r].astype(F32) * o_ref[rw, :], axis=1, keepdims=True) for r, rw in enumerate(slabs)]

        def tile(r, k, v, cl, ce, dq, mask):
            z = lax.dot_general(qs[r], k, _NT, preferred_element_type=F32) * scale
            ls = _log_sigmoid(z)
            lm = ls - z
            if mask is not None:
                lm = jnp.where(mask, lm, 0.0)
            a = jnp.exp(ls + (_dot2(lm, tri) + cl))
            if mask is not None:
                a = jnp.where(mask, a, 0.0)
            e = a * lax.dot_general(dobs[r], v, _NT, preferred_element_type=F32)
            before = etots[r] - (_dot2(e, tri_inc) + ce)
            sig = jnp.exp(ls)
            dz = (e * (1.0 - sig) - sig * before) * scale
            if mask is not None:
                dz = jnp.where(mask, dz, 0.0)
            dzb = dz.astype(BF16)
            dq = dq + jnp.dot(dzb, k, preferred_element_type=F32)
            carry = (cl + jnp.sum(lm, axis=1, keepdims=True), ce + jnp.sum(e, axis=1, keepdims=True), dq)
            return carry, dzb, a.astype(BF16)

        def scatter_kv(j0, dzbs, abs_, which):
            cat = lambda xs: xs[0] if len(xs) == 1 else jnp.concatenate(xs, axis=0)
            qcat, docat = cat([qs[r] for r in which]), cat([dobs[r] for r in which])
            dk_ref[pl.ds(j0, LANE), :] += lax.dot_general(cat(dzbs), qcat, _TN, preferred_element_type=F32)
            dv_ref[pl.ds(j0, LANE), :] += lax.dot_general(cat(abs_), docat, _TN, preferred_element_type=F32)

        zero = jnp.zeros((slab, 1), F32)
        state = [(zero, zero, jnp.zeros((slab, LANE), F32)) for _ in slabs]
        for jj in reversed(range(sub)):
            j0 = pl.multiple_of(i * bq + jj * LANE, LANE)
            k, v = k_ref[pl.ds(j0, LANE), :], v_ref[pl.ds(j0, LANE), :]
            dzbs, abs_, which = [], [], []
            for r, kind in _sb_diagonal(jj, len(slabs), slab):
                state[r], dzb, ab = tile(r, k, v, *state[r], _sb_mask(r, jj, slab) if kind == "mask" else None)
                dzbs.append(dzb)
                abs_.append(ab)
                which.append(r)
            scatter_kv(j0, dzbs, abs_, which)

        def step(t, carry):
            for u in range(SB_UNROLL):
                j0 = pl.multiple_of((i * sub - 1 - (t * SB_UNROLL + u)) * LANE, LANE)
                k, v = k_ref[pl.ds(j0, LANE), :], v_ref[pl.ds(j0, LANE), :]
                outs = [tile(r, k, v, *carry[r], None) for r in range(len(slabs))]
                scatter_kv(j0, [o[1] for o in outs], [o[2] for o in outs], list(range(len(slabs))))
                carry = tuple(o[0] for o in outs)
            return carry

        state = lax.fori_loop(0, (i * sub) // SB_UNROLL, step, tuple(state))
        for r, rw in enumerate(slabs):
            dq_ref[rw, :] = state[r][2]

    tile_spec = pl.BlockSpec((bq, LANE), lambda h, i: (i, h))
    full = pl.BlockSpec((s_len, LANE), lambda h, i: (0, h))
    shp = jax.ShapeDtypeStruct(qn.shape, F32)
    return _call(
        body,
        name="sb_bwd",
        grid=(heads, s_len // bq),
        in_specs=[tile_spec, full, full, tile_spec, tile_spec],
        out_specs=[tile_spec, full, full],
        out_shape=[shp, shp, shp],
        compiler_params=_params("parallel", "arbitrary"),
    )(qn, kn, vb, o, do)


def _shift_down(cur, halo, k):
    if k == 0:
        return cur
    r = pltpu.roll(cur, k, 0)
    p = pltpu.roll(halo, k, 0)
    top = jnp.where(lax.broadcasted_iota(jnp.int32, halo.shape, 0) < k, p, r[:8])
    return jnp.concatenate([top, r[8:]], axis=0)


def _shift_up(cur, halo, k):
    if k == 0:
        return cur
    n = cur.shape[0]
    r = pltpu.roll(cur, n - k, 0)
    p = pltpu.roll(halo, 8 - k, 0)
    bot = jnp.where(lax.broadcasted_iota(jnp.int32, halo.shape, 0) >= 8 - k, p, r[n - 8:])
    return jnp.concatenate([r[: n - 8], bot], axis=0)


def _f_qkv_act(pre, jb, heads):
    act = _silu(pre)
    nrm = act * lax.rsqrt(jnp.sum(act * act, axis=-1, keepdims=True) + EPS)
    nrm = nrm * jnp.where(jb < heads, HEAD_DIM ** -0.5, 1.0)
    return jnp.where(jb < 2 * heads, nrm, act)


def _taps(w_ref):
    return [w_ref[j:j + 1, :] for j in range(CONV_TAPS)]


def _conv_pre(x_ref, w, r0, tr):
    cur = x_ref[pl.ds(r0, tr), :]
    halo = x_ref[pl.ds(jnp.maximum(r0 - 8, 0), 8), :]
    halo = jnp.where(r0 > 0, halo, 0.0)
    shifted = [_shift_down(cur, halo, CONV_TAPS - 1 - j) for j in range(CONV_TAPS)]
    pre = sum(w[j] * shifted[j] for j in range(CONV_TAPS))
    return pre, shifted


def _bprep_fwd(proj, col0, conv_w8, heads, tr):
    s_len = proj.shape[0]
    ncol = 3 * heads
    cb0 = col0 // LANE

    def body(x_ref, w_ref, o_ref):
        jb = pl.program_id(0)
        w = _taps(w_ref)

        def step(t, _):
            r0 = pl.multiple_of(t * tr, tr)
            pre, _unused = _conv_pre(x_ref, w, r0, tr)
            o_ref[pl.ds(r0, tr), :] = _f_qkv_act(pre, jb, heads)
            return 0

        lax.fori_loop(0, s_len // tr, step, 0)

    return _call(
        body,
        name="bprep_fwd",
        grid=(ncol,),
        in_specs=[
            pl.BlockSpec((s_len, LANE), lambda j: (0, cb0 + j)),
            pl.BlockSpec((8, LANE), lambda j: (0, j)),
        ],
        out_specs=pl.BlockSpec((s_len, LANE), lambda j: (0, j)),
        out_shape=jax.ShapeDtypeStruct((s_len, ncol * LANE), F32),
        compiler_params=_params("parallel"),
    )(proj, conv_w8)


def _bprep_bwd(proj, col0, conv_w8, dqkv, heads, tr):
    s_len = proj.shape[0]
    ncol = 3 * heads
    cb0 = col0 // LANE
    nt = s_len // tr

    def body(x_ref, w_ref, d_ref, dx_ref, dw_ref, dpre_ref):
        jb = pl.program_id(0)
        w = _taps(w_ref)

        def pass1(t, dws):
            r0 = pl.multiple_of(t * tr, tr)
            pre, shifted = _conv_pre(x_ref, w, r0, tr)
            _, vjp = jax.vjp(lambda p: _f_qkv_act(p, jb, heads), pre)
            (dpre,) = vjp(d_ref[pl.ds(r0, tr), :])
            dpre_ref[pl.ds(r0, tr), :] = dpre
            return tuple(dws[j] + jnp.sum(dpre * shifted[j], axis=0, keepdims=True) for j in range(CONV_TAPS))

        dws = lax.fori_loop(0, nt, pass1, tuple(jnp.zeros((1, LANE), F32) for _ in range(CONV_TAPS)))
        for j in range(CONV_TAPS):
            dw_ref[j:j + 1, :] = dws[j]
        dw_ref[CONV_TAPS:, :] = jnp.zeros((8 - CONV_TAPS, LANE), F32)

        def pass2(t, _):
            r0 = pl.multiple_of(t * tr, tr)
            cur = dpre_ref[pl.ds(r0, tr), :]
            halo = dpre_ref[pl.ds(jnp.minimum(r0 + tr, s_len - 8), 8), :]
            halo = jnp.where(r0 + tr < s_len, halo, 0.0)
            dx = sum(w[j] * _shift_up(cur, halo, CONV_TAPS - 1 - j) for j in range(CONV_TAPS))
            dx_ref[pl.ds(r0, tr), :] = dx.astype(BF16)
            return 0

        lax.fori_loop(0, nt, pass2, 0)

    col = pl.BlockSpec((s_len, LANE), lambda j: (0, j))
    w_spec = pl.BlockSpec((8, LANE), lambda j: (0, j))
    return _call(
        body,
        name="bprep_bwd",
        grid=(ncol,),
        in_specs=[pl.BlockSpec((s_len, LANE), lambda j: (0, cb0 + j)), w_spec, col],
        out_specs=[col, w_spec],
        out_shape=[jax.ShapeDtypeStruct((s_len, ncol * LANE), BF16), jax.ShapeDtypeStruct((8, ncol * LANE), F32)],
        scratch_shapes=[pltpu.VMEM((s_len, LANE), F32)],
        compiler_params=_params("parallel"),
    )(proj, conv_w8, dqkv)


def _f_gbeta(bt, at, a_log, dt_bias):
    xs = at + dt_bias
    softplus = jnp.maximum(xs, 0.0) + jnp.log(1.0 + jnp.exp(-jnp.abs(xs)))
    return jax.nn.sigmoid(bt), -jnp.exp(a_log) * softplus


def _small(name, fn, ins, out_shapes):
    def body(*refs):
        vals = fn(*[r[...] for r in refs[:len(ins)]])
        for ref, val in zip(refs[len(ins):], vals):
            ref[...] = val.astype(ref.dtype)

    return _call(body, name=name, out_shape=[jax.ShapeDtypeStruct(s, F32) for s in out_shapes],
                 compiler_params=_params())(*ins)


def _dot3(a, b, dims=(((1,), (0,)), ((), ()))):
    ah, bh = a.astype(BF16), b.astype(BF16)
    al, bl = (a - ah.astype(F32)).astype(BF16), (b - bh.astype(F32)).astype(BF16)
    d = lambda u, v: lax.dot_general(u, v, dims, preferred_element_type=F32)
    return d(ah, bh) + (d(ah, bl) + d(al, bh))


def _bdot(a, b, dims=(((1,), (0,)), ((), ()))):
    return lax.dot_general(a.astype(BF16), b.astype(BF16), dims, preferred_element_type=F32)


def _gdn_group(s0, q, k, v, g_row, b_row):
    gs = q.shape[0]
    r = lax.broadcasted_iota(jnp.int32, (gs, gs), 0)
    i = lax.broadcasted_iota(jnp.int32, (gs, gs), 1)
    same = (r // GDN_CHUNK) == (i // GDN_CHUNK)
    g_b = jnp.broadcast_to(g_row, (gs, gs))
    b_b = jnp.broadcast_to(b_row, (gs, gs))
    eye = r == i
    g_col = jnp.sum(jnp.where(eye, g_b, 0.0), axis=1, keepdims=True)
    b_col = jnp.sum(jnp.where(eye, b_b, 0.0), axis=1, keepdims=True)
    gc_col = jnp.sum(jnp.where(same & (i <= r), g_b, 0.0), axis=1, keepdims=True)
    gc_row = jnp.sum(jnp.where(same & (r <= i), jnp.broadcast_to(g_col, (gs, gs)), 0.0), axis=0, keepdims=True)
    gl_col = jnp.sum(jnp.where(same, g_b, 0.0), axis=1, keepdims=True)
    tril = same & (i <= r)
    decay = jnp.where(tril, jnp.exp(jnp.where(tril, gc_col - gc_row, 0.0)), 0.0)
    kk = _bdot(k, k, _NT)
    nmat = jnp.where(same & (i < r), b_col * kk * decay, 0.0)
    am = -nmat
    tinv = jnp.where(eye, 1.0, 0.0) + am
    steps = GDN_CHUNK.bit_length() - 1
    for _ in range(steps - 1):
        am = _dot3(am, am)
        tinv = tinv + _dot3(tinv, am)
    eg = jnp.exp(gc_col)
    w_v = _dot3(tinv, b_col * v)
    w_k = _dot3(tinv, (b_col * eg) * k)
    attn = _bdot(q, k, _NT) * decay
    q_g = q * eg
    k_dec = k * jnp.exp(gl_col - gc_col)
    ridx = lax.broadcasted_iota(jnp.int32, (gs, 1), 0)
    s = s0
    u_all = jnp.zeros_like(v)
    o_inter = jnp.zeros_like(v)
    for c in range(gs // GDN_CHUNK):
        in_chunk = (ridx // GDN_CHUNK) == c
        u = jnp.where(in_chunk, w_v - _bdot(w_k, s), 0.0)
        o_inter = o_inter + jnp.where(in_chunk, _bdot(q_g, s), 0.0)
        u_all = u_all + u
        gl = jnp.sum(jnp.where(ridx == c * GDN_CHUNK, gl_col, 0.0), axis=0, keepdims=True)
        s = jnp.exp(gl) * s + _bdot(k_dec, u, _TN)
    return o_inter + _bdot(attn, u_all), s


GDN_PAIR = 2


def _gdn_specs(heads, ngrp, rev):
    blk = (lambda n: ngrp - 1 - n) if rev else (lambda n: n)
    width = GDN_PAIR * LANE
    def col(off):
        return pl.BlockSpec((GDN_GROUP, width), lambda h, n: (blk(n), (off * heads) // GDN_PAIR + h))
    vec = pl.BlockSpec((GDN_PAIR, ngrp, GDN_GROUP), lambda h, n: (h, 0, 0))
    state = pl.BlockSpec((GDN_PAIR, 1, HEAD_DIM, HEAD_DIM), lambda h, n: (h, blk(n), 0, 0))
    out_col = pl.BlockSpec((GDN_GROUP, width), lambda h, n: (blk(n), h))
    return col, vec, state, out_col


def _gdn_fwd(qkv, g3, b3, heads):
    s_len = qkv.shape[0]
    ngrp = s_len // GDN_GROUP
    col, vec, state, out_col = _gdn_specs(heads, ngrp, False)

    def body(q_ref, k_ref, v_ref, g_ref, b_ref, o_ref, st_ref, s_scr):
        n = pl.program_id(1)

        @pl.when(n == 0)
        def _():
            s_scr[...] = jnp.zeros_like(s_scr)

        for a in range(GDN_PAIR):
            ln = slice(a * LANE, (a + 1) * LANE)
            s0 = s_scr[a]
            st_ref[a, 0] = s0
            o, s1 = _gdn_group(s0, q_ref[:, ln], k_ref[:, ln], v_ref[:, ln],
                               g_ref[a, pl.ds(n, 1), :], b_ref[a, pl.ds(n, 1), :])
            o_ref[:, ln] = o
            s_scr[a] = s1

    return _call(
        body,
        name="gdn_fwd",
        grid=(heads // GDN_PAIR, ngrp),
        in_specs=[col(0), col(1), col(2), vec, vec],
        out_specs=[out_col, state],
        out_shape=[
            jax.ShapeDtypeStruct((s_len, heads * LANE), F32),
            jax.ShapeDtypeStruct((heads, ngrp, HEAD_DIM, HEAD_DIM), F32),
        ],
        scratch_shapes=[pltpu.VMEM((GDN_PAIR, HEAD_DIM, HEAD_DIM), F32)],
        compiler_params=_params("parallel", "arbitrary"),
    )(qkv, qkv, qkv, g3, b3)


def _gdn_bwd(qkv, g3, b3, states, do, heads):
    s_len = qkv.shape[0]
    ngrp = s_len // GDN_GROUP
    col, vec, state, out_col = _gdn_specs(heads, ngrp, True)

    def body(q_ref, k_ref, v_ref, g_ref, b_ref, st_ref, do_ref, dq_ref, dk_ref, dv_ref, dg_ref, db_ref, ds_scr):
        n = pl.program_id(1)
        grp = ngrp - 1 - n

        @pl.when(n == 0)
        def _():
            ds_scr[...] = jnp.zeros_like(ds_scr)

        for a in range(GDN_PAIR):
            ln = slice(a * LANE, (a + 1) * LANE)
            _, vjp = jax.vjp(_gdn_group, st_ref[a, 0], q_ref[:, ln], k_ref[:, ln], v_ref[:, ln],
                             g_ref[a, pl.ds(grp, 1), :], b_ref[a, pl.ds(grp, 1), :])
            ds0, dq, dk, dv, dg, db = vjp((do_ref[:, ln], ds_scr[a]))
            dq_ref[:, ln] = dq
            dk_ref[:, ln] = dk
            dv_ref[:, ln] = dv
            dg_ref[a, pl.ds(grp, 1), :] = dg
            db_ref[a, pl.ds(grp, 1), :] = db
            ds_scr[a] = ds0

    shp = jax.ShapeDtypeStruct((s_len, heads * LANE), F32)
    vshp = jax.ShapeDtypeStruct(g3.shape, F32)
    return _call(
        body,
        name="gdn_bwd",
        grid=(heads // GDN_PAIR, ngrp),
        in_specs=[col(0), col(1), col(2), vec, vec, state, out_col],
        out_specs=[out_col, out_col, out_col, vec, vec],
        out_shape=[shp, shp, shp, vshp, vshp],
        scratch_shapes=[pltpu.VMEM((GDN_PAIR, HEAD_DIM, HEAD_DIM), F32)],
        compiler_params=_params("parallel", "arbitrary"),
    )(qkv, qkv, qkv, g3, b3, states, do)


def _sum_adam(name, parts, w, m, v, tr):
    n, rows, cols = parts.shape

    def body(p_ref, w_ref, m_ref, v_ref, g_out, d_out, m_out, v_out):
        g = p_ref[0].astype(F32)
        for s in range(1, n):
            g = g + p_ref[s].astype(F32)
        m_new = ADAM_B1 * m_ref[...] + (1.0 - ADAM_B1) * g
        v_new = ADAM_B2 * v_ref[...] + (1.0 - ADAM_B2) * (g * g)
        m_hat = m_new / (1.0 - ADAM_B1 ** ADAM_STEP)
        v_hat = v_new / (1.0 - ADAM_B2 ** ADAM_STEP)
        g_out[...] = g
        d_out[...] = -ADAM_LR * (m_hat / (jnp.sqrt(v_hat) + ADAM_EPS) + ADAM_WD * w_ref[...])
        m_out[...] = m_new
        v_out[...] = v_new

    mat = pl.BlockSpec((tr, cols), lambda i: (i, 0))
    shp = jax.ShapeDtypeStruct((rows, cols), F32)
    return _call(
        body,
        name=name,
        grid=(rows // tr,),
        in_specs=[pl.BlockSpec((n, tr, cols), lambda i: (0, i, 0)), mat, mat, mat],
        out_specs=[mat, mat, mat, mat],
        out_shape=[shp, shp, shp, shp],
        compiler_params=_params("parallel"),
    )(parts, w, m, v)


def _sum_parts(name, parts):
    n = parts.shape[0]

    def fn(p):
        g = p[0]
        for s in range(1, n):
            g = g + p[s]
        return (g,)

    return _small(name, fn, [parts], [parts.shape[1:]])[0]


def _pad_rows(a, mult):
    r = (-a.shape[0]) % mult
    return a if r == 0 else jnp.concatenate([a, jnp.zeros((r,) + a.shape[1:], a.dtype)], axis=0)


def _pad_lanes(a, width):
    return jnp.concatenate([a, jnp.zeros(a.shape[:-1] + (width - a.shape[-1],), a.dtype)], axis=-1)


def _pack(pieces, width, mult):
    return _pad_rows(jnp.concatenate([p.reshape(-1, width) for p in pieces], axis=0), mult)


def _unpack(packed, shapes, width):
    out, r0 = [], 0
    for shp in shapes:
        size = 1
        for d in shp:
            size *= d
        nr = size // width
        out.append(packed[..., r0:r0 + nr, :].reshape(packed.shape[:-2] + tuple(shp)))
        r0 += nr
    return out


def kernel(x, c, w_mod, b_mod, norm1_w, w_in, q_norm_w, k_norm_w, conv_w, a_log, dt_bias, o_norm_w, p_a, p_b, w_out, norm2_w, w_gate, w_up, w_down, loss_target, m_w_mod, m_b_mod, m_norm1_w, m_w_in, m_q_norm_w, m_k_norm_w, m_conv_w, m_a_log, m_dt_bias, m_o_norm_w, m_p_a, m_p_b, m_w_out, m_norm2_w, m_w_gate, m_w_up, m_w_down, v_w_mod, v_b_mod, v_norm1_w, v_w_in, v_q_norm_w, v_k_norm_w, v_conv_w, v_a_log, v_dt_bias, v_o_norm_w, v_p_a, v_p_b, v_w_out, v_norm2_w, v_w_gate, v_w_up, v_w_down):
    s_len, d = x.shape[1], x.shape[2]
    heads = a_log.shape[1]
    dh = heads * HEAD_DIM
    f = w_down.shape[1] * N_DEV
    din_loc = w_in.shape[2]
    me = _my_id()
    x2, tgt = x[0], loss_target[0]

    big = [w_in, p_a, p_b, w_out, w_gate, w_up, w_down]
    big_shapes = [t.shape[1:] for t in big]
    wg = _gather_two_level("gather_weights", _pack([t[0].astype(BF16) for t in big], d, LANE))
    g_in, g_pa, g_pb, g_out, g_gate, g_up, g_down = _unpack(wg, big_shapes, d)
    cols = lambda t: t.transpose(1, 0, 2).reshape(t.shape[1], -1)
    rows = lambda t: t.reshape(-1, t.shape[2])
    w_in_g = cols(g_in)
    o_ba = 3 * dh + 3 * dh + dh
    din = w_in_g.shape[1]
    n_perm = o_ba + 2 * d + LANE
    w_in_p = jnp.concatenate(
        [w_in_g[:, :o_ba], w_in_g[:, o_ba + 2 * heads:], w_in_g[:, o_ba:o_ba + 2 * heads],
         jnp.zeros((d, LANE - 2 * heads), BF16)], axis=1)
    o_qkvb, o_z, o_ga, o_bad = 3 * dh, 6 * dh, 7 * dh, 7 * dh + 2 * d
    w_pa, w_pb, w_o, w_dn = rows(g_pa), rows(g_pb), rows(g_out), rows(g_down)
    w_gu = jnp.concatenate([cols(g_gate), cols(g_up)], axis=1)

    c_all = _pad_rows(_exchange("gather_c", c, False).reshape(N_DEV, d), LANE)
    mod_part = _mm("mod_fwd", c_all, w_mod[0], a_fn=_silu, tk=d)[:N_DEV]
    mod_all = _exchange("gather_mod", mod_part, False)
    mod_me = lax.dynamic_index_in_dim(mod_all, me, axis=1, keepdims=False).reshape(1, 6 * d) + b_mod
    sh1, sc1, gt1, sh2, sc2, gt2 = [mod_me[:, j * d:(j + 1) * d] for j in range(6)]

    tm = min(128, s_len)
    th = min(1024, s_len)
    (u1,) = _seg("pre1_fwd", lambda a, nw, sc, sh: (_f_mod(a, nw, sc, sh),),
                 [_row(x2), _par(norm1_w), _par(sc1), _par(sh1)], [(d, BF16, "row", d, 0, 0)], s_len, tm)
    proj = _mm("proj", u1, w_in_p, tm=1024, tn=640, tk=d)

    hb = dh // LANE
    qn, kn, vb = _seg(
        "qk_fwd", lambda q, k, v, qw, kw: _f_qk(q, k, qw, kw) + (v,),
        [_row(proj, LANE, 0, 1), _row(proj, LANE, hb, 1), _row(proj, LANE, 2 * hb, 1), _par(q_norm_w), _par(k_norm_w)],
        [(dh, BF16, "row", LANE, 0, 1)] * 3, s_len, th, nh=hb)
    bq = min(512, s_len)
    o_a = _sb_fwd(qn, kn, vb, heads, bq)

    conv_all = _exchange("gather_conv", conv_w[0], False)
    conv_g = _pad_rows(conv_all.transpose(1, 0, 2).reshape(CONV_TAPS, 3 * dh), 8)
    tr = min(512, s_len)
    qkv_b = _bprep_fwd(proj, o_qkvb, conv_g, heads, tr)
    ba_t = proj[:, o_bad:o_bad + 2 * heads].T
    a_col, dt_col = a_log.reshape(heads, 1), dt_bias.reshape(heads, 1)
    beta_t, g_t = _small("gbeta_fwd", _f_gbeta, [ba_t[:heads], ba_t[heads:], a_col, dt_col], [(heads, s_len)] * 2)
    ngrp = s_len // GDN_GROUP
    g3, b3 = g_t.reshape(heads, ngrp, GDN_GROUP), beta_t.reshape(heads, ngrp, GDN_GROUP)
    o_b, states = _gdn_fwd(qkv_b, g3, b3, heads)
    zoff = o_z // LANE
    (ob2,) = _seg("post_fwd", lambda ob, z, ow: (_f_post(ob, z, ow),),
                  [_row(o_b, LANE, 0, 1), _row(proj, LANE, zoff, 1), _par(o_norm_w)],
                  [(dh, BF16, "row", LANE, 0, 1)], s_len, th, nh=hb)

    ya = _mm("ya", o_a, w_pa, tm=1024, tk=dh)
    yb = _mm("yb", ob2, w_pb, tm=1024, tk=dh)
    goff = o_ga // d
    gate_ins = [_row(proj, d, goff, 0), _row(proj, d, goff + 1, 0)]
    (merged,) = _seg("merge_fwd", lambda a, b, ga, gb: (_f_merge(a, b, ga, gb),),
                     [_row(ya), _row(yb)] + gate_ins, [(d, BF16, "row", d, 0, 0)], s_len, tm)
    t_out = _mm("attn_out", merged, w_o, tm=1024, tk=d)
    res_par = [_par(gt1), _par(norm2_w), _par(sc2), _par(sh2)]
    h1, u2 = _seg("res1_fwd", _f_res, [_row(x2), _row(t_out)] + res_par,
                  [(d, F32, "row", d, 0, 0), (d, BF16, "row", d, 0, 0)], s_len, tm)
    gu = _mm("ffn_in", u2, w_gu, tm=1024, tk=d)
    tf = 128
    (ff,) = _seg("ff_fwd", lambda a, b: (_f_ff(a, b),), [_row(gu, f, 0, 0), _row(gu, f, 1, 0)],
                 [(f, BF16, "row", f, 0, 0)], s_len, tf)
    dn = _mm("ffn_out", ff, w_dn, tm=1024, tk=512)

    def f_loss(hh, dd, g2, tg):
        err = hh + g2 * dd - tg
        dh2 = err * (1.0 / d)
        return (dh2, g2 * dh2, jnp.sum(err * err, axis=0, keepdims=True), jnp.sum(dh2 * dd, axis=0, keepdims=True))

    dh2, ddn, lsq, dgt2 = _seg(
        "loss", f_loss, [_row(h1), _row(dn), _par(gt2), _row(tgt)],
        [(d, F32, "row", d, 0, 0), (d, BF16, "row", d, 0, 0), (d, F32, "acc", d, 0, 0), (d, F32, "acc", d, 0, 0)],
        s_len, tm)
    loss = lax.psum(0.5 * jnp.sum(lsq) / d, MESH_AXES)

    dff = _mm("d_ff", ddn, w_dn, tb=True, tm=1024, tk=d)
    dw_down = _mm("dw_down", ff, ddn, ta=True, out_dtype=BF16, tk=1024)

    def f_ff_bwd(a, b, dy):
        _, vjp = jax.vjp(_f_ff, a, b)
        da, db = vjp(dy)
        return (jnp.concatenate([da, db], axis=1),)

    (dgu,) = _seg("ff_bwd", f_ff_bwd, [_row(gu, f, 0, 0), _row(gu, f, 1, 0), _row(dff)],
                  [(2 * f, BF16, "row", 2 * f, 0, 0)], s_len, tf)
    du2 = _mm("d_u2", dgu, w_gu, tb=True, tm=1024, tk=1024)
    dw_gu = _mm("dw_gu", u2, dgu, ta=True, out_dtype=BF16, tk=1024)

    def f_res_bwd(a, t, g, nw, sc, sh, dhd, du):
        _, vjp = jax.vjp(_f_res, a, t, g, nw, sc, sh)
        da, dt_, dg, dnw, dsc, dsh = vjp((dhd, du))
        return da, dt_, dg, dnw, dsc, dsh

    acc_d = (d, F32, "acc", d, 0, 0)
    dh1, dt_out, dgt1, dnorm2, dsc2, dsh2 = _seg(
        "res1_bwd", f_res_bwd, [_row(x2), _row(t_out)] + res_par + [_row(dh2), _row(du2)],
        [(d, F32, "row", d, 0, 0), (d, BF16, "row", d, 0, 0), acc_d, acc_d, acc_d, acc_d], s_len, tm)

    dmerged = _mm("d_merged", dt_out, w_o, tb=True, tm=1024, tk=d)
    dw_out = _mm("dw_out", merged, dt_out, ta=True, out_dtype=BF16, tk=1024)

    def f_merge_bwd(a, b, ga, gb, dy):
        _, vjp = jax.vjp(_f_merge, a, b, ga, gb)
        da, db, dga, dgb = vjp(dy)
        return da, db, jnp.concatenate([dga, dgb], axis=1)

    dya, dyb, dgates = _seg("merge_bwd", f_merge_bwd, [_row(ya), _row(yb)] + gate_ins + [_row(dmerged)],
                            [(d, BF16, "row", d, 0, 0), (d, BF16, "row", d, 0, 0), (2 * d, BF16, "row", 2 * d, 0, 0)],
                            s_len, tm)
    do_a = _mm("d_oa", dya, w_pa, tb=True, tm=1024, tk=d)
    dw_pa = _mm("dw_pa", o_a, dya, ta=True, out_dtype=BF16, tk=1024)
    dob2 = _mm("d_ob2", dyb, w_pb, tb=True, tm=1024, tk=d)
    dw_pb = _mm("dw_pb", ob2, dyb, ta=True, out_dtype=BF16, tk=1024)

    def f_post_bwd(ob, z, ow, dy):
        _, vjp = jax.vjp(_f_post, ob, z, ow)
        return vjp(dy)

    acc_h = (LANE, F32, "acc", LANE, 0, 0)
    d_ob, dz, d_onorm = _seg(
        "post_bwd", f_post_bwd,
        [_row(o_b, LANE, 0, 1), _row(proj, LANE, zoff, 1), _par(o_norm_w), _row(dob2, LANE, 0, 1)],
        [(dh, F32, "row", LANE, 0, 1), (dh, BF16, "row", LANE, 0, 1), acc_h], s_len, th, nh=hb)
    dqb, dkb, dvb, dg3, db3 = _gdn_bwd(qkv_b, g3, b3, states, d_ob, heads)
    dqkv_n = jnp.concatenate([dqb, dkb, dvb], axis=1)
    dqkv_pre, dconv8 = _bprep_bwd(proj, o_qkvb, conv_g, dqkv_n, heads, tr)

    def f_gbeta_bwd(bt, at, al, dtb, dbeta, dg):
        _, vjp = jax.vjp(_f_gbeta, bt, at, al, dtb)
        return vjp((dbeta, dg))

    dbt, dat, da_log, ddt = _small(
        "gbeta_bwd", f_gbeta_bwd,
        [ba_t[:heads], ba_t[heads:], a_col, dt_col, db3.reshape(heads, s_len), dg3.reshape(heads, s_len)],
        [(heads, s_len), (heads, s_len), (heads, 1), (heads, 1)])
    dba = _pad_lanes(jnp.concatenate([dbt, dat], axis=0).T, LANE).astype(BF16)

    dqn, dkn, dva = _sb_bwd(qn, kn, vb, o_a, do_a, heads, bq)

    def f_qk_bwd(q, k, qw, kw, dq, dk):
        _, vjp = jax.vjp(_f_qk, q, k, qw, kw)
        return vjp((dq, dk))

    dqa, dka, d_qnorm, d_knorm = _seg(
        "qk_bwd", f_qk_bwd,
        [_row(proj, LANE, 0, 1), _row(proj, LANE, hb, 1), _par(q_norm_w), _par(k_norm_w),
         _row(dqn, LANE, 0, 1), _row(dkn, LANE, 0, 1)],
        [(dh, BF16, "row", LANE, 0, 1), (dh, BF16, "row", LANE, 0, 1), acc_h, acc_h], s_len, th, nh=hb)

    dproj = jnp.concatenate([dqa, dka, dva.astype(BF16), dqkv_pre, dz, dgates, dba], axis=1)
    du1 = _mm("d_u1", dproj, w_in_p, tb=True, tm=1024, tk=640)
    dw_in_p = _mm("dw_in", u1, dproj, ta=True, out_dtype=BF16, tn=640, tk=1024)

    def f_pre_bwd(a, nw, sc, sh, du, dres):
        _, vjp = jax.vjp(_f_mod, a, nw, sc, sh)
        da, dnw, dsc, dsh = vjp(du)
        return da + dres, dnw, dsc, dsh

    dx, dnorm1, dsc1, dsh1 = _seg(
        "pre1_bwd", f_pre_bwd, [_row(x2), _par(norm1_w), _par(sc1), _par(sh1), _row(du1), _row(dh1)],
        [(d, F32, "row", d, 0, 0), acc_d, acc_d, acc_d], s_len, tm)

    dmod = jnp.concatenate([dsh1, dsc1, dgt1, dsh2, dsc2, dgt2], axis=1)
    lanes = lambda t: _pad_lanes(t.reshape(1, -1), LANE)
    small = [dmod, dnorm1, d_qnorm, d_knorm, lanes(da_log), lanes(ddt), d_onorm, dnorm2, dconv8[:CONV_TAPS]]
    small_shapes = [t.shape for t in small]
    small_all = _exchange("gather_small", _pack(small, LANE, 8), False)
    small_sum = _sum_parts("sum_small", small_all)
    g_bmod, g_n1, g_qn, g_kn, g_al, g_dt, g_on, g_n2, g_conv = _unpack(small_sum, small_shapes, LANE)
    ncv = conv_w.shape[2]
    g_conv_me = lax.dynamic_slice_in_dim(g_conv, me * ncv, ncv, axis=1)

    nmod = w_mod.shape[2]
    dmod_all = _unpack(small_all, small_shapes[:1], LANE)[0].reshape(N_DEV, 6 * d)
    dmod_cols = _pad_rows(lax.dynamic_slice_in_dim(dmod_all, me * nmod, nmod, axis=1), LANE)
    g_wmod = _mm("dw_mod", c_all, dmod_cols, ta=True, a_fn=_silu, tk=LANE)

    dw_in_g = jnp.concatenate([dw_in_p[:, :o_ba], dw_in_p[:, o_bad:o_bad + 2 * heads], dw_in_p[:, o_ga:o_ga + 2 * d]], axis=1)
    to_cols = lambda t: t.reshape(t.shape[0], N_DEV, -1).transpose(1, 0, 2)
    to_rows = lambda t: t.reshape(N_DEV, -1, t.shape[1])
    parts = [to_cols(dw_in_g), to_rows(dw_pa), to_rows(dw_pb), to_rows(dw_out),
             to_cols(dw_gu[:, :f]), to_cols(dw_gu[:, f:]), to_rows(dw_down)]
    send = jnp.concatenate([p.reshape(N_DEV, -1, d) for p in parts], axis=1)
    pad = (-send.shape[1]) % LANE
    if pad:
        send = jnp.concatenate([send, jnp.zeros((N_DEV, pad, d), BF16)], axis=1)
    recv = _exchange("scatter_grads", send, True)

    def adam_packed(name, parts_, ws, ms, vs, width, mult, tr_):
        outs = _sum_adam(name, parts_, _pack(ws, width, mult), _pack(ms, width, mult), _pack(vs, width, mult), tr_)
        shapes = [t.shape for t in ws]
        return [_unpack(o, shapes, width) for o in outs]

    big_m = [m_w_in, m_p_a, m_p_b, m_w_out, m_w_gate, m_w_up, m_w_down]
    big_v = [v_w_in, v_p_a, v_p_b, v_w_out, v_w_gate, v_w_up, v_w_down]
    gb, db_, mb, vb_ = adam_packed("adam_big", recv, big, big_m, big_v, d, LANE, LANE)

    gm, dm_, mm_, vm_ = adam_packed("adam_mod", g_wmod[None], [w_mod], [m_w_mod], [v_w_mod], nmod, 8, _tile(d, LANE))

    sm_w = [b_mod, norm1_w, q_norm_w, k_norm_w, lanes(a_log), lanes(dt_bias), o_norm_w, norm2_w, conv_w[0]]
    sm_m = [m_b_mod, m_norm1_w, m_q_norm_w, m_k_norm_w, lanes(m_a_log), lanes(m_dt_bias), m_o_norm_w, m_norm2_w, m_conv_w[0]]
    sm_v = [v_b_mod, v_norm1_w, v_q_norm_w, v_k_norm_w, lanes(v_a_log), lanes(v_dt_bias), v_o_norm_w, v_norm2_w, v_conv_w[0]]
    sm_g = [g_bmod, g_n1, g_qn, g_kn, g_al, g_dt, g_on, g_n2, g_conv_me]
    g_pack = _pack(sm_g, LANE, 8)
    gs_, ds_, ms_, vs_ = adam_packed("adam_small", g_pack[None], sm_w, sm_m, sm_v, LANE, 8, g_pack.shape[0])

    def assemble(big_l, mod_l, small_l):
        s_bmod, s_n1, s_qn, s_kn, s_al, s_dt, s_on, s_n2, s_conv = small_l
        b_in, b_pa, b_pb, b_out, b_gate, b_up, b_down = big_l
        return [mod_l[0], s_bmod, s_n1, b_in, s_qn, s_kn, s_conv[None], s_al[:, :heads], s_dt[:, :heads], s_on,
                b_pa, b_pb, b_out, s_n2, b_gate, b_up, b_down]

    outs = [loss, dx[None]]
    for big_l, mod_l, small_l in ((gb, gm, gs_), (db_, dm_, ds_), (mb, mm_, ms_), (vb_, vm_, vs_)):
        outs += assemble(big_l, mod_l, small_l)
    return tuple(outs)
```

```python
import jax
import jax.numpy as jnp
from jax import lax
from jax.experimental import pallas as pl
from jax.experimental.pallas import tpu as pltpu

F32, BF16 = jnp.float32, jnp.bfloat16
EPS = 1e-6
HEAD_DIM = 128
GDN_CHUNK = 64
GDN_GROUP = 256
CONV_TAPS = 4
N_DEV = 8
MESH_AXES = ("x", "y", "c")
ADAM_LR, ADAM_B1, ADAM_B2, ADAM_EPS, ADAM_WD, ADAM_STEP = 0.001, 0.9, 0.999, 1e-08, 0.01, 10
VMEM_LIMIT_BYTES = 56 * 1024 * 1024
LANE = 128
MESH_ID = pl.DeviceIdType.MESH


def _call(body, **kw):
    return pl.pallas_call(body, **kw)


def _params(*sem):
    return pltpu.CompilerParams(dimension_semantics=sem or None, vmem_limit_bytes=VMEM_LIMIT_BYTES)


def _tile(n, target):
    t = (min(n, target) // LANE) * LANE
    while t >= LANE:
        if n % t == 0:
            return t
        t -= LANE
    return n


def _silu(x):
    return x * jax.nn.sigmoid(x)


def _my_id():
    return 4 * lax.axis_index("x") + 2 * lax.axis_index("y") + lax.axis_index("c")


def _exchange(name, src, scatter):
    blk = src.shape[1:] if scatter else src.shape

    def body(src_ref, out_ref, send_sems, recv_sems, local_sem):
        x, y, c = lax.axis_index("x"), lax.axis_index("y"), lax.axis_index("c")
        me = 4 * x + 2 * y + c

        def peer(k):
            kx, ky, kc = (k >> 2) & 1, (k >> 1) & 1, k & 1
            px, py, pc = x ^ kx, y ^ ky, c ^ kc
            return (px, py, pc), 4 * px + 2 * py + pc

        def copy(k):
            dev, pid = peer(k)
            return pltpu.make_async_remote_copy(
                src_ref=src_ref.at[pid] if scatter else src_ref,
                dst_ref=out_ref.at[me],
                send_sem=send_sems.at[k - 1],
                recv_sem=recv_sems.at[k - 1],
                device_id=dev,
                device_id_type=MESH_ID,
            )

        def arrival(k):
            dev, pid = peer(k)
            return pltpu.make_async_remote_copy(
                src_ref=src_ref.at[pid] if scatter else src_ref,
                dst_ref=out_ref.at[pid],
                send_sem=send_sems.at[k - 1],
                recv_sem=recv_sems.at[k - 1],
                device_id=dev,
                device_id_type=MESH_ID,
            )

        mine = pltpu.make_async_copy(src_ref.at[me] if scatter else src_ref, out_ref.at[me], local_sem)
        mine.start()
        for k in range(1, N_DEV):
            copy(k).start()
        for k in range(1, N_DEV):
            arrival(k).wait_recv()
        for k in range(1, N_DEV):
            copy(k).wait_send()
        mine.wait()

    return _call(
        body,
        name=name,
        out_shape=jax.ShapeDtypeStruct((N_DEV,) + tuple(blk), src.dtype),
        in_specs=[pl.BlockSpec(memory_space=pl.ANY)],
        out_specs=pl.BlockSpec(memory_space=pl.ANY),
        scratch_shapes=[
            pltpu.SemaphoreType.DMA((N_DEV - 1,)),
            pltpu.SemaphoreType.DMA((N_DEV - 1,)),
            pltpu.SemaphoreType.DMA,
        ],
    )(src)


def _gather_two_level(name, src):
    def body(src_ref, out_ref, send_sems, recv_sems, local_sem):
        x, y, c = lax.axis_index("x"), lax.axis_index("y"), lax.axis_index("c")
        me, sibling = (x, y, c), (x, y, 1 - c)
        chips = [(1 - x, y), (x, 1 - y), (1 - x, 1 - y)]

        def slot(px, py, pc):
            return out_ref.at[4 * px + 2 * py + pc]

        def copy(k, block, to, from_src=False):
            return pltpu.make_async_remote_copy(
                src_ref=src_ref if from_src else slot(*block),
                dst_ref=slot(*block),
                send_sem=send_sems.at[k],
                recv_sem=recv_sems.at[k],
                device_id=to,
                device_id_type=MESH_ID,
            )

        mine = pltpu.make_async_copy(src_ref, slot(*me), local_sem)
        mine.start()
        first = [copy(0, me, sibling, True)] + [copy(1 + j, me, (*chip, c), True) for j, chip in enumerate(chips)]
        for cp in first:
            cp.start()
        passed = [copy(4 + j, (*chip, c), sibling) for j, chip in enumerate(chips)]
        for j, chip in enumerate(chips):
            copy(1 + j, (*chip, c), me).wait_recv()
            passed[j].start()
        copy(0, sibling, me).wait_recv()
        for j, chip in enumerate(chips):
            copy(4 + j, (*chip, 1 - c), me).wait_recv()
        for cp in first + passed:
            cp.wait_send()
        mine.wait()

    return _call(
        body,
        name=name,
        out_shape=jax.ShapeDtypeStruct((N_DEV,) + tuple(src.shape), src.dtype),
        in_specs=[pl.BlockSpec(memory_space=pl.ANY)],
        out_specs=pl.BlockSpec(memory_space=pl.ANY),
        scratch_shapes=[
            pltpu.SemaphoreType.DMA((N_DEV - 1,)),
            pltpu.SemaphoreType.DMA((N_DEV - 1,)),
            pltpu.SemaphoreType.DMA,
        ],
    )(src)


def _mm(name, a, b, *, ta=False, tb=False, out_dtype=F32, a_fn=None, tm=512, tn=512, tk=512):
    m, kdim = (a.shape[1], a.shape[0]) if ta else a.shape
    n = b.shape[0] if tb else b.shape[1]
    assert kdim == (b.shape[1] if tb else b.shape[0]), (a.shape, b.shape, ta, tb)
    tm, tn, tk = _tile(m, tm), _tile(n, tn), _tile(kdim, tk)
    nk = kdim // tk
    a_spec = pl.BlockSpec((tk, tm), lambda i, j, k: (k, i)) if ta else pl.BlockSpec((tm, tk), lambda i, j, k: (i, k))
    b_spec = pl.BlockSpec((tn, tk), lambda i, j, k: (j, k)) if tb else pl.BlockSpec((tk, tn), lambda i, j, k: (k, j))
    dims = (((0 if ta else 1,), (1 if tb else 0,)), ((), ()))

    def body(a_ref, b_ref, o_ref, acc_ref):
        k = pl.program_id(2)
        av = a_ref[...]
        if a_fn is not None:
            av = a_fn(av.astype(F32))
        p = lax.dot_general(av.astype(BF16), b_ref[...].astype(BF16), dims, preferred_element_type=F32)
        if nk == 1:
            o_ref[...] = p.astype(out_dtype)
        else:
            @pl.when(k == 0)
            def _():
                acc_ref[...] = p

            @pl.when(k > 0)
            def _():
                acc_ref[...] += p

            @pl.when(k == nk - 1)
            def _():
                o_ref[...] = acc_ref[...].astype(out_dtype)

    return _call(
        body,
        name=name,
        grid=(m // tm, n // tn, nk),
        in_specs=[a_spec, b_spec],
        out_specs=pl.BlockSpec((tm, tn), lambda i, j, k: (i, j)),
        out_shape=jax.ShapeDtypeStruct((m, n), out_dtype),
        scratch_shapes=[pltpu.VMEM((tm, tn) if nk > 1 else (8, LANE), F32)],
        compiler_params=_params("parallel", "parallel", "arbitrary"),
    )(a, b)


def _row(arr, w=None, off=0, stride=0):
    return (arr, "row", arr.shape[1] if w is None else w, off, stride)


def _par(arr, w=None, off=0, stride=0):
    return (arr, "par", arr.shape[1] if w is None else w, off, stride)


def _seg(name, fn, ins, outs, rows, tm, nh=1):
    nrow = rows // tm

    def spec(kind, w, off, stride):
        if kind == "row":
            return pl.BlockSpec((tm, w), lambda h, i: (i, off + stride * h))
        return pl.BlockSpec((1, w), lambda h, i: (0, off + stride * h))

    in_specs = [spec(kind, w, off, stride) for (_, kind, w, off, stride) in ins]
    out_specs = [spec("row" if kind == "row" else "par", w, off, stride) for (_, _, kind, w, off, stride) in outs]
    out_shape = [
        jax.ShapeDtypeStruct((rows if kind == "row" else 1, ncols), dt) for (ncols, dt, kind, _, _, _) in outs
    ]
    n_in = len(ins)

    def body(*refs):
        h, i = pl.program_id(0), pl.program_id(1)
        vals = fn(*[r[...] for r in refs[:n_in]])
        for (_, dt, kind, _, _, stride), ref, val in zip(outs, refs[n_in:], vals):
            if kind == "row":
                ref[...] = val.astype(dt)
            else:
                first = (i == 0) if stride != 0 else jnp.logical_and(i == 0, h == 0)

                @pl.when(first)
                def _(ref=ref, val=val):
                    ref[...] = val.astype(F32)

                @pl.when(jnp.logical_not(first))
                def _(ref=ref, val=val):
                    ref[...] += val.astype(F32)

    res = _call(
        body,
        name=name,
        grid=(nh, nrow),
        in_specs=in_specs,
        out_specs=out_specs,
        out_shape=out_shape,
        compiler_params=_params("arbitrary", "arbitrary"),
    )(*[t[0] for t in ins])
    return res


def _rmsn(x, w):
    return x * lax.rsqrt(jnp.mean(x * x, axis=-1, keepdims=True) + EPS) * w


def _f_mod(x, nw, sc, sh):
    return _rmsn(x, nw) * (1.0 + sc) + sh


def _f_res(x, t, g, nw, sc, sh):
    h = x + g * t
    return h, _f_mod(h, nw, sc, sh)


def _f_ff(gg, uu):
    return _silu(gg) * uu


def _f_merge(ya, yb, ga, gb):
    return jax.nn.sigmoid(ga) * ya + jax.nn.sigmoid(gb) * yb


def _f_qk(q, k, qw, kw):
    return _rmsn(q, qw), _rmsn(k, kw)


def _f_post(ob, z, ow):
    return _rmsn(ob, ow) * _silu(z)


def _f32(*xs):
    return [x.astype(F32) for x in xs]


def _log_sigmoid(z):
    return jnp.minimum(z, 0.0) - jnp.log(1.0 + jnp.exp(-jnp.abs(z)))


def _dot2(x, tri):
    hi = x.astype(BF16)
    lo = (x - hi.astype(F32)).astype(BF16)
    return jnp.dot(hi, tri, preferred_element_type=F32) + jnp.dot(lo, tri, preferred_element_type=F32)


def _tri(strict):
    j = lax.broadcasted_iota(jnp.int32, (LANE, LANE), 0)
    s = lax.broadcasted_iota(jnp.int32, (LANE, LANE), 1)
    return ((j > s) if strict else (j >= s)).astype(BF16)


SB_SLAB = 512
SB_UNROLL = 4


def _sb_slabs(bq, slab):
    return [slice(r * slab, (r + 1) * slab) for r in range(bq // slab)]


def _sb_diagonal(jj, n_slabs, slab):
    out = []
    for r in range(n_slabs):
        if jj * LANE >= (r + 1) * slab:
            continue
        out.append((r, "full" if (jj + 1) * LANE <= r * slab else "mask"))
    return out


def _sb_mask(r, jj, slab):
    t = r * slab + lax.broadcasted_iota(jnp.int32, (slab, LANE), 0)
    s = jj * LANE + lax.broadcasted_iota(jnp.int32, (slab, LANE), 1)
    return s < t


_NT = (((1,), (1,)), ((), ()))
_TN = (((0,), (0,)), ((), ()))


def _sb_fwd(qn, kn, vb, heads, bq):
    s_len = qn.shape[0]
    scale = HEAD_DIM ** -0.5
    sub = bq // LANE
    slab = min(SB_SLAB, bq)

    def body(q_ref, k_ref, v_ref, o_ref):
        i = pl.program_id(1)
        tri = _tri(True)
        slabs = _sb_slabs(bq, slab)
        qs = [q_ref[rw, :] for rw in slabs]

        def tile(q, k, v, cl, acc, mask):
            z = lax.dot_general(q, k, _NT, preferred_element_type=F32) * scale
            ls = _log_sigmoid(z)
            lm = ls - z
            if mask is not None:
                lm = jnp.where(mask, lm, 0.0)
            w = jnp.exp(ls + (_dot2(lm, tri) + cl))
            if mask is not None:
                w = jnp.where(mask, w, 0.0)
            return cl + jnp.sum(lm, axis=1, keepdims=True), acc + _dot2(w, v)

        state = [(jnp.zeros((slab, 1), F32), jnp.zeros((slab, LANE), F32)) for _ in slabs]
        for jj in reversed(range(sub)):
            j0 = pl.multiple_of(i * bq + jj * LANE, LANE)
            k, v = k_ref[pl.ds(j0, LANE), :], v_ref[pl.ds(j0, LANE), :]
            for r, kind in _sb_diagonal(jj, len(slabs), slab):
                state[r] = tile(qs[r], k, v, *state[r], _sb_mask(r, jj, slab) if kind == "mask" else None)

        def step(t, carry):
            for u in range(SB_UNROLL):
                j0 = pl.multiple_of((i * sub - 1 - (t * SB_UNROLL + u)) * LANE, LANE)
                k, v = k_ref[pl.ds(j0, LANE), :], v_ref[pl.ds(j0, LANE), :]
                carry = tuple(tile(qs[r], k, v, *carry[r], None) for r in range(len(slabs)))
            return carry

        state = lax.fori_loop(0, (i * sub) // SB_UNROLL, step, tuple(state))
        for r, rw in enumerate(slabs):
            o_ref[rw, :] = state[r][1]

    return _call(
        body,
        name="sb_fwd",
        grid=(heads, s_len // bq),
        in_specs=[
            pl.BlockSpec((bq, LANE), lambda h, i: (i, h)),
            pl.BlockSpec((s_len, LANE), lambda h, i: (0, h)),
            pl.BlockSpec((s_len, LANE), lambda h, i: (0, h)),
        ],
        out_specs=pl.BlockSpec((bq, LANE), lambda h, i: (i, h)),
        out_shape=jax.ShapeDtypeStruct(qn.shape, F32),
        compiler_params=_params("parallel", "arbitrary"),
    )(qn, kn, vb)


def _sb_bwd(qn, kn, vb, o, do, heads, bq):
    s_len = qn.shape[0]
    scale = HEAD_DIM ** -0.5
    sub = bq // LANE
    slab = min(SB_SLAB, bq)

    def body(q_ref, k_ref, v_ref, o_ref, do_ref, dq_ref, dk_ref, dv_ref):
        i = pl.program_id(1)

        @pl.when(i == 0)
        def _():
            dk_ref[...] = jnp.zeros_like(dk_ref)
            dv_ref[...] = jnp.zeros_like(dv_ref)

        tri, tri_inc = _tri(True), _tri(False)
        slabs = _sb_slabs(bq, slab)
        qs = [q_ref[rw, :] for rw in slabs]
        dobs = [do_ref[rw, :].astype(BF16) for rw in slabs]
        etots = [jnp.sum(dobs[r].astype(F32) * o_ref[rw, :], axis=1, keepdims=True) for r, rw in enumerate(slabs)]

        def tile(r, k, v, cl, ce, dq, mask):
            z = lax.dot_general(qs[r], k, _NT, preferred_element_type=F32) * scale
            ls = _log_sigmoid(z)
            lm = ls - z
            if mask is not None:
                lm = jnp.where(mask, lm, 0.0)
            a = jnp.exp(ls + (_dot2(lm, tri) + cl))
            if mask is not None:
                a = jnp.where(mask, a, 0.0)
            e = a * lax.dot_general(dobs[r], v, _NT, preferred_element_type=F32)
            before = etots[r] - (_dot2(e, tri_inc) + ce)
            sig = jnp.exp(ls)
            dz = (e * (1.0 - sig) - sig * before) * scale
            if mask is not None:
                dz = jnp.where(mask, dz, 0.0)
            dzb = dz.astype(BF16)
            dq = dq + jnp.dot(dzb, k, preferred_element_type=F32)
            carry = (cl + jnp.sum(lm, axis=1, keepdims=True), ce + jnp.sum(e, axis=1, keepdims=True), dq)
            return carry, dzb, a.astype(BF16)

        def scatter_kv(j0, dzbs, abs_, which):
            cat = lambda xs: xs[0] if len(xs) == 1 else jnp.concatenate(xs, axis=0)
            qcat, docat = cat([qs[r] for r in which]), cat([dobs[r] for r in which])
            dk_ref[pl.ds(j0, LANE), :] += lax.dot_general(cat(dzbs), qcat, _TN, preferred_element_type=F32)
            dv_ref[pl.ds(j0, LANE), :] += lax.dot_general(cat(abs_), docat, _TN, preferred_element_type=F32)

        zero = jnp.zeros((slab, 1), F32)
        state = [(zero, zero, jnp.zeros((slab, LANE), F32)) for _ in slabs]
        for jj in reversed(range(sub)):
            j0 = pl.multiple_of(i * bq + jj * LANE, LANE)
            k, v = k_ref[pl.ds(j0, LANE), :], v_ref[pl.ds(j0, LANE), :]
            dzbs, abs_, which = [], [], []
            for r, kind in _sb_diagonal(jj, len(slabs), slab):
                state[r], dzb, ab = tile(r, k, v, *state[r], _sb_mask(r, jj, slab) if kind == "mask" else None)
                dzbs.append(dzb)
                abs_.append(ab)
                which.append(r)
            scatter_kv(j0, dzbs, abs_, which)

        def step(t, carry):
            for u in range(SB_UNROLL):
                j0 = pl.multiple_of((i * sub - 1 - (t * SB_UNROLL + u)) * LANE, LANE)
                k, v = k_ref[pl.ds(j0, LANE), :], v_ref[pl.ds(j0, LANE), :]
                outs = [tile(r, k, v, *carry[r], None) for r in range(len(slabs))]
                scatter_kv(j0, [o[1] for o in outs], [o[2] for o in outs], list(range(len(slabs))))
                carry = tuple(o[0] for o in outs)
            return carry

        state = lax.fori_loop(0, (i * sub) // SB_UNROLL, step, tuple(state))
        for r, rw in enumerate(slabs):
            dq_ref[rw, :] = state[r][2]

    tile_spec = pl.BlockSpec((bq, LANE), lambda h, i: (i, h))
    full = pl.BlockSpec((s_len, LANE), lambda h, i: (0, h))
    shp = jax.ShapeDtypeStruct(qn.shape, F32)
    return _call(
        body,
        name="sb_bwd",
        grid=(heads, s_len // bq),
        in_specs=[tile_spec, full, full, tile_spec, tile_spec],
        out_specs=[tile_spec, full, full],
        out_shape=[shp, shp, shp],
        compiler_params=_params("parallel", "arbitrary"),
    )(qn, kn, vb, o, do)


def _shift_down(cur, halo, k):
    if k == 0:
        return cur
    r = pltpu.roll(cur, k, 0)
    p = pltpu.roll(halo, k, 0)
    top = jnp.where(lax.broadcasted_iota(jnp.int32, halo.shape, 0) < k, p, r[:8])
    return jnp.concatenate([top, r[8:]], axis=0)


def _shift_up(cur, halo, k):
    if k == 0:
        return cur
    n = cur.shape[0]
    r = pltpu.roll(cur, n - k, 0)
    p = pltpu.roll(halo, 8 - k, 0)
    bot = jnp.where(lax.broadcasted_iota(jnp.int32, halo.shape, 0) >= 8 - k, p, r[n - 8:])
    return jnp.concatenate([r[: n - 8], bot], axis=0)


def _f_qkv_act(pre, jb, heads):
    act = _silu(pre)
    nrm = act * lax.rsqrt(jnp.sum(act * act, axis=-1, keepdims=True) + EPS)
    nrm = nrm * jnp.where(jb < heads, HEAD_DIM ** -0.5, 1.0)
    return jnp.where(jb < 2 * heads, nrm, act)


def _taps(w_ref):
    return [w_ref[j:j + 1, :] for j in range(CONV_TAPS)]


def _conv_pre(x_ref, w, r0, tr):
    cur = x_ref[pl.ds(r0, tr), :]
    halo = x_ref[pl.ds(jnp.maximum(r0 - 8, 0), 8), :]
    halo = jnp.where(r0 > 0, halo, 0.0)
    shifted = [_shift_down(cur, halo, CONV_TAPS - 1 - j) for j in range(CONV_TAPS)]
    pre = sum(w[j] * shifted[j] for j in range(CONV_TAPS))
    return pre, shifted


def _bprep_fwd(proj, col0, conv_w8, heads, tr):
    s_len = proj.shape[0]
    ncol = 3 * heads
    cb0 = col0 // LANE

    def body(x_ref, w_ref, o_ref):
        jb = pl.program_id(0)
        w = _taps(w_ref)

        def step(t, _):
            r0 = pl.multiple_of(t * tr, tr)
            pre, _unused = _conv_pre(x_ref, w, r0, tr)
            o_ref[pl.ds(r0, tr), :] = _f_qkv_act(pre, jb, heads)
            return 0

        lax.fori_loop(0, s_len // tr, step, 0)

    return _call(
        body,
        name="bprep_fwd",
        grid=(ncol,),
        in_specs=[
            pl.BlockSpec((s_len, LANE), lambda j: (0, cb0 + j)),
            pl.BlockSpec((8, LANE), lambda j: (0, j)),
        ],
        out_specs=pl.BlockSpec((s_len, LANE), lambda j: (0, j)),
        out_shape=jax.ShapeDtypeStruct((s_len, ncol * LANE), F32),
        compiler_params=_params("parallel"),
    )(proj, conv_w8)


def _bprep_bwd(proj, col0, conv_w8, dqkv, heads, tr):
    s_len = proj.shape[0]
    ncol = 3 * heads
    cb0 = col0 // LANE
    nt = s_len // tr

    def body(x_ref, w_ref, d_ref, dx_ref, dw_ref, dpre_ref):
        jb = pl.program_id(0)
        w = _taps(w_ref)

        def pass1(t, dws):
            r0 = pl.multiple_of(t * tr, tr)
            pre, shifted = _conv_pre(x_ref, w, r0, tr)
            _, vjp = jax.vjp(lambda p: _f_qkv_act(p, jb, heads), pre)
            (dpre,) = vjp(d_ref[pl.ds(r0, tr), :])
            dpre_ref[pl.ds(r0, tr), :] = dpre
            return tuple(dws[j] + jnp.sum(dpre * shifted[j], axis=0, keepdims=True) for j in range(CONV_TAPS))

        dws = lax.fori_loop(0, nt, pass1, tuple(jnp.zeros((1, LANE), F32) for _ in range(CONV_TAPS)))
        for j in range(CONV_TAPS):
            dw_ref[j:j + 1, :] = dws[j]
        dw_ref[CONV_TAPS:, :] = jnp.zeros((8 - CONV_TAPS, LANE), F32)

        def pass2(t, _):
            r0 = pl.multiple_of(t * tr, tr)
            cur = dpre_ref[pl.ds(r0, tr), :]
            halo = dpre_ref[pl.ds(jnp.minimum(r0 + tr, s_len - 8), 8), :]
            halo = jnp.where(r0 + tr < s_len, halo, 0.0)
            dx = sum(w[j] * _shift_up(cur, halo, CONV_TAPS - 1 - j) for j in range(CONV_TAPS))
            dx_ref[pl.ds(r0, tr), :] = dx.astype(BF16)
            return 0

        lax.fori_loop(0, nt, pass2, 0)

    col = pl.BlockSpec((s_len, LANE), lambda j: (0, j))
    w_spec = pl.BlockSpec((8, LANE), lambda j: (0, j))
    return _call(
        body,
        name="bprep_bwd",
        grid=(ncol,),
        in_specs=[pl.BlockSpec((s_len, LANE), lambda j: (0, cb0 + j)), w_spec, col],
        out_specs=[col, w_spec],
        out_shape=[jax.ShapeDtypeStruct((s_len, ncol * LANE), BF16), jax.ShapeDtypeStruct((8, ncol * LANE), F32)],
        scratch_shapes=[pltpu.VMEM((s_len, LANE), F32)],
        compiler_params=_params("parallel"),
    )(proj, conv_w8, dqkv)


def _f_gbeta(bt, at, a_log, dt_bias):
    xs = at + dt_bias
    softplus = jnp.maximum(xs, 0.0) + jnp.log(1.0 + jnp.exp(-jnp.abs(xs)))
    return jax.nn.sigmoid(bt), -jnp.exp(a_log) * softplus


def _small(name, fn, ins, out_shapes):
    def body(*refs):
        vals = fn(*[r[...] for r in refs[:len(ins)]])
        for ref, val in zip(refs[len(ins):], vals):
            ref[...] = val.astype(ref.dtype)

    return _call(body, name=name, out_shape=[jax.ShapeDtypeStruct(s, F32) for s in out_shapes],
                 compiler_params=_params())(*ins)


def _dot3(a, b, dims=(((1,), (0,)), ((), ()))):
    ah, bh = a.astype(BF16), b.astype(BF16)
    al, bl = (a - ah.astype(F32)).astype(BF16), (b - bh.astype(F32)).astype(BF16)
    d = lambda u, v: lax.dot_general(u, v, dims, preferred_element_type=F32)
    return d(ah, bh) + (d(ah, bl) + d(al, bh))


def _bdot(a, b, dims=(((1,), (0,)), ((), ()))):
    return lax.dot_general(a.astype(BF16), b.astype(BF16), dims, preferred_element_type=F32)


@jax.custom_vjp
def _unit_lower_inverse(nmat):
    r = lax.broadcasted_iota(jnp.int32, nmat.shape, 0)
    i = lax.broadcasted_iota(jnp.int32, nmat.shape, 1)
    am = -nmat
    tinv = jnp.where(r == i, 1.0, 0.0) + am
    for _ in range(GDN_CHUNK.bit_length() - 2):
        am = _dot3(am, am)
        tinv = tinv + _dot3(tinv, am)
    return tinv


def _unit_lower_inverse_fwd(nmat):
    tinv = _unit_lower_inverse(nmat)
    return tinv, tinv


def _unit_lower_inverse_bwd(tinv, dt):
    return (-_dot3(_dot3(tinv, dt, _TN), tinv, _NT),)


_unit_lower_inverse.defvjp(_unit_lower_inverse_fwd, _unit_lower_inverse_bwd)


def _gdn_group(s0, q, k, v, g_row, b_row):
    gs = q.shape[0]
    r = lax.broadcasted_iota(jnp.int32, (gs, gs), 0)
    i = lax.broadcasted_iota(jnp.int32, (gs, gs), 1)
    same = (r // GDN_CHUNK) == (i // GDN_CHUNK)
    g_b = jnp.broadcast_to(g_row, (gs, gs))
    b_b = jnp.broadcast_to(b_row, (gs, gs))
    eye = r == i
    g_col = jnp.sum(jnp.where(eye, g_b, 0.0), axis=1, keepdims=True)
    b_col = jnp.sum(jnp.where(eye, b_b, 0.0), axis=1, keepdims=True)
    gc_col = jnp.sum(jnp.where(same & (i <= r), g_b, 0.0), axis=1, keepdims=True)
    gc_row = jnp.sum(jnp.where(same & (r <= i), jnp.broadcast_to(g_col, (gs, gs)), 0.0), axis=0, keepdims=True)
    gl_col = jnp.sum(jnp.where(same, g_b, 0.0), axis=1, keepdims=True)
    tril = same & (i <= r)
    decay = jnp.where(tril, jnp.exp(jnp.where(tril, gc_col - gc_row, 0.0)), 0.0)
    kk = _bdot(k, k, _NT)
    nmat = jnp.where(same & (i < r), b_col * kk * decay, 0.0)
    tinv = _unit_lower_inverse(nmat)
    eg = jnp.exp(gc_col)
    w_v = _dot3(tinv, b_col * v)
    w_k = _dot3(tinv, (b_col * eg) * k)
    attn = _bdot(q, k, _NT) * decay
    q_g = q * eg
    k_dec = k * jnp.exp(gl_col - gc_col)
    ridx = lax.broadcasted_iota(jnp.int32, (gs, 1), 0)
    s = s0
    u_all = jnp.zeros_like(v)
    o_inter = jnp.zeros_like(v)
    for c in range(gs // GDN_CHUNK):
        in_chunk = (ridx // GDN_CHUNK) == c
        u = jnp.where(in_chunk, w_v - _bdot(w_k, s), 0.0)
        o_inter = o_inter + jnp.where(in_chunk, _bdot(q_g, s), 0.0)
        u_all = u_all + u
        gl = jnp.sum(jnp.where(ridx == c * GDN_CHUNK, gl_col, 0.0), axis=0, keepdims=True)
        s = jnp.exp(gl) * s + _bdot(k_dec, u, _TN)
    return o_inter + _bdot(attn, u_all), s


GDN_PAIR = 2


def _gdn_specs(heads, ngrp, rev):
    blk = (lambda n: ngrp - 1 - n) if rev else (lambda n: n)
    width = GDN_PAIR * LANE
    def col(off):
        return pl.BlockSpec((GDN_GROUP, width), lambda h, n: (blk(n), (off * heads) // GDN_PAIR + h))
    vec = pl.BlockSpec((GDN_PAIR, ngrp, GDN_GROUP), lambda h, n: (h, 0, 0))
    state = pl.BlockSpec((GDN_PAIR, 1, HEAD_DIM, HEAD_DIM), lambda h, n: (h, blk(n), 0, 0))
    out_col = pl.BlockSpec((GDN_GROUP, width), lambda h, n: (blk(n), h))
    return col, vec, state, out_col


def _gdn_fwd(qkv, g3, b3, heads):
    s_len = qkv.shape[0]
    ngrp = s_len // GDN_GROUP
    col, vec, state, out_col = _gdn_specs(heads, ngrp, False)

    def body(q_ref, k_ref, v_ref, g_ref, b_ref, o_ref, st_ref, s_scr):
        n = pl.program_id(1)

        @pl.when(n == 0)
        def _():
            s_scr[...] = jnp.zeros_like(s_scr)

        for a in range(GDN_PAIR):
            ln = slice(a * LANE, (a + 1) * LANE)
            s0 = s_scr[a]
            st_ref[a, 0] = s0
            o, s1 = _gdn_group(s0, q_ref[:, ln], k_ref[:, ln], v_ref[:, ln],
                               g_ref[a, pl.ds(n, 1), :], b_ref[a, pl.ds(n, 1), :])
            o_ref[:, ln] = o
            s_scr[a] = s1

    return _call(
        body,
        name="gdn_fwd",
        grid=(heads // GDN_PAIR, ngrp),
        in_specs=[col(0), col(1), col(2), vec, vec],
        out_specs=[out_col, state],
        out_shape=[
            jax.ShapeDtypeStruct((s_len, heads * LANE), F32),
            jax.ShapeDtypeStruct((heads, ngrp, HEAD_DIM, HEAD_DIM), F32),
        ],
        scratch_shapes=[pltpu.VMEM((GDN_PAIR, HEAD_DIM, HEAD_DIM), F32)],
        compiler_params=_params("parallel", "arbitrary"),
    )(qkv, qkv, qkv, g3, b3)


def _gdn_bwd(qkv, g3, b3, states, do, heads):
    s_len = qkv.shape[0]
    ngrp = s_len // GDN_GROUP
    col, vec, state, out_col = _gdn_specs(heads, ngrp, True)

    def body(q_ref, k_ref, v_ref, g_ref, b_ref, st_ref, do_ref, dq_ref, dk_ref, dv_ref, dg_ref, db_ref, ds_scr):
        n = pl.program_id(1)
        grp = ngrp - 1 - n

        @pl.when(n == 0)
        def _():
            ds_scr[...] = jnp.zeros_like(ds_scr)

        for a in range(GDN_PAIR):
            ln = slice(a * LANE, (a + 1) * LANE)
            _, vjp = jax.vjp(_gdn_group, st_ref[a, 0], q_ref[:, ln], k_ref[:, ln], v_ref[:, ln],
                             g_ref[a, pl.ds(grp, 1), :], b_ref[a, pl.ds(grp, 1), :])
            ds0, dq, dk, dv, dg, db = vjp((do_ref[:, ln], ds_scr[a]))
            dq_ref[:, ln] = dq
            dk_ref[:, ln] = dk
            dv_ref[:, ln] = dv
            dg_ref[a, pl.ds(grp, 1), :] = dg
            db_ref[a, pl.ds(grp, 1), :] = db
            ds_scr[a] = ds0

    shp = jax.ShapeDtypeStruct((s_len, heads * LANE), F32)
    vshp = jax.ShapeDtypeStruct(g3.shape, F32)
    return _call(
        body,
        name="gdn_bwd",
        grid=(heads // GDN_PAIR, ngrp),
        in_specs=[col(0), col(1), col(2), vec, vec, state, out_col],
        out_specs=[out_col, out_col, out_col, vec, vec],
        out_shape=[shp, shp, shp, vshp, vshp],
        scratch_shapes=[pltpu.VMEM((GDN_PAIR, HEAD_DIM, HEAD_DIM), F32)],
        compiler_params=_params("parallel", "arbitrary"),
    )(qkv, qkv, qkv, g3, b3, states, do)


def _sum_adam(name, parts, w, m, v, tr):
    n, rows, cols = parts.shape

    def body(p_ref, w_ref, m_ref, v_ref, g_out, d_out, m_out, v_out):
        g = p_ref[0].astype(F32)
        for s in range(1, n):
            g = g + p_ref[s].astype(F32)
        m_new = ADAM_B1 * m_ref[...] + (1.0 - ADAM_B1) * g
        v_new = ADAM_B2 * v_ref[...] + (1.0 - ADAM_B2) * (g * g)
        m_hat = m_new / (1.0 - ADAM_B1 ** ADAM_STEP)
        v_hat = v_new / (1.0 - ADAM_B2 ** ADAM_STEP)
        g_out[...] = g
        d_out[...] = -ADAM_LR * (m_hat / (jnp.sqrt(v_hat) + ADAM_EPS) + ADAM_WD * w_ref[...])
        m_out[...] = m_new
        v_out[...] = v_new

    mat = pl.BlockSpec((tr, cols), lambda i: (i, 0))
    shp = jax.ShapeDtypeStruct((rows, cols), F32)
    return _call(
        body,
        name=name,
        grid=(rows // tr,),
        in_specs=[pl.BlockSpec((n, tr, cols), lambda i: (0, i, 0)), mat, mat, mat],
        out_specs=[mat, mat, mat, mat],
        out_shape=[shp, shp, shp, shp],
        compiler_params=_params("parallel"),
    )(parts, w, m, v)


def _sum_parts(name, parts):
    n = parts.shape[0]

    def fn(p):
        g = p[0]
        for s in range(1, n):
            g = g + p[s]
        return (g,)

    return _small(name, fn, [parts], [parts.shape[1:]])[0]


def _pad_rows(a, mult):
    r = (-a.shape[0]) % mult
    return a if r == 0 else jnp.concatenate([a, jnp.zeros((r,) + a.shape[1:], a.dtype)], axis=0)


def _pad_lanes(a, width):
    return jnp.concatenate([a, jnp.zeros(a.shape[:-1] + (width - a.shape[-1],), a.dtype)], axis=-1)


def _pack(pieces, width, mult):
    return _pad_rows(jnp.concatenate([p.reshape(-1, width) for p in pieces], axis=0), mult)


def _unpack(packed, shapes, width):
    out, r0 = [], 0
    for shp in shapes:
        size = 1
        for d in shp:
            size *= d
        nr = size // width
        out.append(packed[..., r0:r0 + nr, :].reshape(packed.shape[:-2] + tuple(shp)))
        r0 += nr
    return out


def kernel(x, c, w_mod, b_mod, norm1_w, w_in, q_norm_w, k_norm_w, conv_w, a_log, dt_bias, o_norm_w, p_a, p_b, w_out, norm2_w, w_gate, w_up, w_down, loss_target, m_w_mod, m_b_mod, m_norm1_w, m_w_in, m_q_norm_w, m_k_norm_w, m_conv_w, m_a_log, m_dt_bias, m_o_norm_w, m_p_a, m_p_b, m_w_out, m_norm2_w, m_w_gate, m_w_up, m_w_down, v_w_mod, v_b_mod, v_norm1_w, v_w_in, v_q_norm_w, v_k_norm_w, v_conv_w, v_a_log, v_dt_bias, v_o_norm_w, v_p_a, v_p_b, v_w_out, v_norm2_w, v_w_gate, v_w_up, v_w_down):
    s_len, d = x.shape[1], x.shape[2]
    heads = a_log.shape[1]
    dh = heads * HEAD_DIM
    f = w_down.shape[1] * N_DEV
    din_loc = w_in.shape[2]
    me = _my_id()
    x2, tgt = x[0], loss_target[0]

    big = [w_in, p_a, p_b, w_out, w_gate, w_up, w_down]
    big_shapes = [t.shape[1:] for t in big]
    wg = _gather_two_level("gather_weights", _pack([t[0].astype(BF16) for t in big], d, LANE))
    g_in, g_pa, g_pb, g_out, g_gate, g_up, g_down = _unpack(wg, big_shapes, d)
    cols = lambda t: t.transpose(1, 0, 2).reshape(t.shape[1], -1)
    rows = lambda t: t.reshape(-1, t.shape[2])
    w_in_g = cols(g_in)
    o_ba = 3 * dh + 3 * dh + dh
    din = w_in_g.shape[1]
    n_perm = o_ba + 2 * d + LANE
    w_in_p = jnp.concatenate(
        [w_in_g[:, :o_ba], w_in_g[:, o_ba + 2 * heads:], w_in_g[:, o_ba:o_ba + 2 * heads],
         jnp.zeros((d, LANE - 2 * heads), BF16)], axis=1)
    o_qkvb, o_z, o_ga, o_bad = 3 * dh, 6 * dh, 7 * dh, 7 * dh + 2 * d
    w_pa, w_pb, w_o, w_dn = rows(g_pa), rows(g_pb), rows(g_out), rows(g_down)
    w_gu = jnp.concatenate([cols(g_gate), cols(g_up)], axis=1)

    c_all = _pad_rows(_exchange("gather_c", c, False).reshape(N_DEV, d), LANE)
    mod_part = _mm("mod_fwd", c_all, w_mod[0], a_fn=_silu, tk=d)[:N_DEV]
    mod_all = _exchange("gather_mod", mod_part, False)
    mod_me = lax.dynamic_index_in_dim(mod_all, me, axis=1, keepdims=False).reshape(1, 6 * d) + b_mod
    sh1, sc1, gt1, sh2, sc2, gt2 = [mod_me[:, j * d:(j + 1) * d] for j in range(6)]

    tm = min(128, s_len)
    th = min(1024, s_len)
    (u1,) = _seg("pre1_fwd", lambda a, nw, sc, sh: (_f_mod(a, nw, sc, sh),),
                 [_row(x2), _par(norm1_w), _par(sc1), _par(sh1)], [(d, BF16, "row", d, 0, 0)], s_len, tm)
    proj = _mm("proj", u1, w_in_p, tm=1024, tn=640, tk=d)

    hb = dh // LANE
    qn, kn, vb = _seg(
        "qk_fwd", lambda q, k, v, qw, kw: _f_qk(q, k, qw, kw) + (v,),
        [_row(proj, LANE, 0, 1), _row(proj, LANE, hb, 1), _row(proj, LANE, 2 * hb, 1), _par(q_norm_w), _par(k_norm_w)],
        [(dh, BF16, "row", LANE, 0, 1)] * 3, s_len, th, nh=hb)
    bq = min(512, s_len)
    o_a = _sb_fwd(qn, kn, vb, heads, bq)

    conv_all = _exchange("gather_conv", conv_w[0], False)
    conv_g = _pad_rows(conv_all.transpose(1, 0, 2).reshape(CONV_TAPS, 3 * dh), 8)
    tr = min(512, s_len)
    qkv_b = _bprep_fwd(proj, o_qkvb, conv_g, heads, tr)
    ba_t = proj[:, o_bad:o_bad + 2 * heads].T
    a_col, dt_col = a_log.reshape(heads, 1), dt_bias.reshape(heads, 1)
    beta_t, g_t = _small("gbeta_fwd", _f_gbeta, [ba_t[:heads], ba_t[heads:], a_col, dt_col], [(heads, s_len)] * 2)
    ngrp = s_len // GDN_GROUP
    g3, b3 = g_t.reshape(heads, ngrp, GDN_GROUP), beta_t.reshape(heads, ngrp, GDN_GROUP)
    o_b, states = _gdn_fwd(qkv_b, g3, b3, heads)
    zoff = o_z // LANE
    (ob2,) = _seg("post_fwd", lambda ob, z, ow: (_f_post(ob, z, ow),),
                  [_row(o_b, LANE, 0, 1), _row(proj, LANE, zoff, 1), _par(o_norm_w)],
                  [(dh, BF16, "row", LANE, 0, 1)], s_len, th, nh=hb)

    ya = _mm("ya", o_a, w_pa, tm=1024, tk=dh)
    yb = _mm("yb", ob2, w_pb, tm=1024, tk=dh)
    goff = o_ga // d
    gate_ins = [_row(proj, d, goff, 0), _row(proj, d, goff + 1, 0)]
    (merged,) = _seg("merge_fwd", lambda a, b, ga, gb: (_f_merge(a, b, ga, gb),),
                     [_row(ya), _row(yb)] + gate_ins, [(d, BF16, "row", d, 0, 0)], s_len, tm)
    t_out = _mm("attn_out", merged, w_o, tm=1024, tk=d)
    res_par = [_par(gt1), _par(norm2_w), _par(sc2), _par(sh2)]
    h1, u2 = _seg("res1_fwd", _f_res, [_row(x2), _row(t_out)] + res_par,
                  [(d, F32, "row", d, 0, 0), (d, BF16, "row", d, 0, 0)], s_len, tm)
    gu = _mm("ffn_in", u2, w_gu, tm=1024, tk=d)
    tf = 128
    (ff,) = _seg("ff_fwd", lambda a, b: (_f_ff(a, b),), [_row(gu, f, 0, 0), _row(gu, f, 1, 0)],
                 [(f, BF16, "row", f, 0, 0)], s_len, tf)
    dn = _mm("ffn_out", ff, w_dn, tm=1024, tk=f // 2)

    def f_loss(hh, dd, g2, tg):
        err = hh + g2 * dd - tg
        dh2 = err * (1.0 / d)
        return (dh2, g2 * dh2, jnp.sum(err * err, axis=0, keepdims=True), jnp.sum(dh2 * dd, axis=0, keepdims=True))

    dh2, ddn, lsq, dgt2 = _seg(
        "loss", f_loss, [_row(h1), _row(dn), _par(gt2), _row(tgt)],
        [(d, F32, "row", d, 0, 0), (d, BF16, "row", d, 0, 0), (d, F32, "acc", d, 0, 0), (d, F32, "acc", d, 0, 0)],
        s_len, tm)
    loss = lax.psum(0.5 * jnp.sum(lsq) / d, MESH_AXES)

    dff = _mm("d_ff", ddn, w_dn, tb=True, tm=1024, tk=d)
    dw_down = _mm("dw_down", ff, ddn, ta=True, out_dtype=BF16, tk=2048)

    def f_ff_bwd(a, b, dy):
        _, vjp = jax.vjp(_f_ff, a, b)
        da, db = vjp(dy)
        return (jnp.concatenate([da, db], axis=1),)

    (dgu,) = _seg("ff_bwd", f_ff_bwd, [_row(gu, f, 0, 0), _row(gu, f, 1, 0), _row(dff)],
                  [(2 * f, BF16, "row", 2 * f, 0, 0)], s_len, tf)
    du2 = _mm("d_u2", dgu, w_gu, tb=True, tm=1024, tk=f // 2)
    dw_gu = _mm("dw_gu", u2, dgu, ta=True, out_dtype=BF16, tm=1024, tk=2048)

    def f_res_bwd(a, t, g, nw, sc, sh, dhd, du):
        _, vjp = jax.vjp(_f_res, a, t, g, nw, sc, sh)
        da, dt_, dg, dnw, dsc, dsh = vjp((dhd, du))
        return da, dt_, dg, dnw, dsc, dsh

    acc_d = (d, F32, "acc", d, 0, 0)
    dh1, dt_out, dgt1, dnorm2, dsc2, dsh2 = _seg(
        "res1_bwd", f_res_bwd, [_row(x2), _row(t_out)] + res_par + [_row(dh2), _row(du2)],
        [(d, F32, "row", d, 0, 0), (d, BF16, "row", d, 0, 0), acc_d, acc_d, acc_d, acc_d], s_len, tm)

    dmerged = _mm("d_merged", dt_out, w_o, tb=True, tm=1024, tk=d)
    dw_out = _mm("dw_out", merged, dt_out, ta=True, out_dtype=BF16, tm=1024, tk=2048)

    def f_merge_bwd(a, b, ga, gb, dy):
        _, vjp = jax.vjp(_f_merge, a, b, ga, gb)
        da, db, dga, dgb = vjp(dy)
        return da, db, jnp.concatenate([dga, dgb], axis=1)

    dya, dyb, dgates = _seg("merge_bwd", f_merge_bwd, [_row(ya), _row(yb)] + gate_ins + [_row(dmerged)],
                            [(d, BF16, "row", d, 0, 0), (d, BF16, "row", d, 0, 0), (2 * d, BF16, "row", 2 * d, 0, 0)],
                            s_len, tm)
    do_a = _mm("d_oa", dya, w_pa, tb=True, tm=1024, tk=d)
    dw_pa = _mm("dw_pa", o_a, dya, ta=True, out_dtype=BF16, tk=2048)
    dob2 = _mm("d_ob2", dyb, w_pb, tb=True, tm=1024, tk=d)
    dw_pb = _mm("dw_pb", ob2, dyb, ta=True, out_dtype=BF16, tm=1024, tk=2048)

    def f_post_bwd(ob, z, ow, dy):
        _, vjp = jax.vjp(_f_post, ob, z, ow)
        return vjp(dy)

    acc_h = (LANE, F32, "acc", LANE, 0, 0)
    d_ob, dz, d_onorm = _seg(
        "post_bwd", f_post_bwd,
        [_row(o_b, LANE, 0, 1), _row(proj, LANE, zoff, 1), _par(o_norm_w), _row(dob2, LANE, 0, 1)],
        [(dh, F32, "row", LANE, 0, 1), (dh, BF16, "row", LANE, 0, 1), acc_h], s_len, th, nh=hb)
    dqb, dkb, dvb, dg3, db3 = _gdn_bwd(qkv_b, g3, b3, states, d_ob, heads)
    dqkv_n = jnp.concatenate([dqb, dkb, dvb], axis=1)
    dqkv_pre, dconv8 = _bprep_bwd(proj, o_qkvb, conv_g, dqkv_n, heads, tr)

    def f_gbeta_bwd(bt, at, al, dtb, dbeta, dg):
        _, vjp = jax.vjp(_f_gbeta, bt, at, al, dtb)
        return vjp((dbeta, dg))

    dbt, dat, da_log, ddt = _small(
        "gbeta_bwd", f_gbeta_bwd,
        [ba_t[:heads], ba_t[heads:], a_col, dt_col, db3.reshape(heads, s_len), dg3.reshape(heads, s_len)],
        [(heads, s_len), (heads, s_len), (heads, 1), (heads, 1)])
    dba = _pad_lanes(jnp.concatenate([dbt, dat], axis=0).T, LANE).astype(BF16)

    dqn, dkn, dva = _sb_bwd(qn, kn, vb, o_a, do_a, heads, bq)

    def f_qk_bwd(q, k, qw, kw, dq, dk):
        _, vjp = jax.vjp(_f_qk, q, k, qw, kw)
        return vjp((dq, dk))

    dqa, dka, d_qnorm, d_knorm = _seg(
        "qk_bwd", f_qk_bwd,
        [_row(proj, LANE, 0, 1), _row(proj, LANE, hb, 1), _par(q_norm_w), _par(k_norm_w),
         _row(dqn, LANE, 0, 1), _row(dkn, LANE, 0, 1)],
        [(dh, BF16, "row", LANE, 0, 1), (dh, BF16, "row", LANE, 0, 1), acc_h, acc_h], s_len, th, nh=hb)

    dproj = jnp.concatenate([dqa, dka, dva.astype(BF16), dqkv_pre, dz, dgates, dba], axis=1)
    du1 = _mm("d_u1", dproj, w_in_p, tb=True, tm=1024, tk=n_perm // 5)
    dw_in_p = _mm("dw_in", u1, dproj, ta=True, out_dtype=BF16, tm=1024, tn=640, tk=2048)

    def f_pre_bwd(a, nw, sc, sh, du, dres):
        _, vjp = jax.vjp(_f_mod, a, nw, sc, sh)
        da, dnw, dsc, dsh = vjp(du)
        return da + dres, dnw, dsc, dsh

    dx, dnorm1, dsc1, dsh1 = _seg(
        "pre1_bwd", f_pre_bwd, [_row(x2), _par(norm1_w), _par(sc1), _par(sh1), _row(du1), _row(dh1)],
        [(d, F32, "row", d, 0, 0), acc_d, acc_d, acc_d], s_len, tm)

    dmod = jnp.concatenate([dsh1, dsc1, dgt1, dsh2, dsc2, dgt2], axis=1)
    lanes = lambda t: _pad_lanes(t.reshape(1, -1), LANE)
    small = [dmod, dnorm1, d_qnorm, d_knorm, lanes(da_log), lanes(ddt), d_onorm, dnorm2, dconv8[:CONV_TAPS]]
    small_shapes = [t.shape for t in small]
    small_all = _exchange("gather_small", _pack(small, LANE, 8), False)
    small_sum = _sum_parts("sum_small", small_all)
    g_bmod, g_n1, g_qn, g_kn, g_al, g_dt, g_on, g_n2, g_conv = _unpack(small_sum, small_shapes, LANE)
    ncv = conv_w.shape[2]
    g_conv_me = lax.dynamic_slice_in_dim(g_conv, me * ncv, ncv, axis=1)

    nmod = w_mod.shape[2]
    dmod_all = _unpack(small_all, small_shapes[:1], LANE)[0].reshape(N_DEV, 6 * d)
    dmod_cols = _pad_rows(lax.dynamic_slice_in_dim(dmod_all, me * nmod, nmod, axis=1), LANE)
    g_wmod = _mm("dw_mod", c_all, dmod_cols, ta=True, a_fn=_silu, tk=LANE)

    dw_in_g = jnp.concatenate([dw_in_p[:, :o_ba], dw_in_p[:, o_bad:o_bad + 2 * heads], dw_in_p[:, o_ga:o_ga + 2 * d]], axis=1)
    to_cols = lambda t: t.reshape(t.shape[0], N_DEV, -1).transpose(1, 0, 2)
    to_rows = lambda t: t.reshape(N_DEV, -1, t.shape[1])
    parts = [to_cols(dw_in_g), to_rows(dw_pa), to_rows(dw_pb), to_rows(dw_out),
             to_cols(dw_gu[:, :f]), to_cols(dw_gu[:, f:]), to_rows(dw_down)]
    send = jnp.concatenate([p.reshape(N_DEV, -1, d) for p in parts], axis=1)
    pad = (-send.shape[1]) % LANE
    if pad:
        send = jnp.concatenate([send, jnp.zeros((N_DEV, pad, d), BF16)], axis=1)
    recv = _exchange("scatter_grads", send, True)

    def adam_packed(name, parts_, ws, ms, vs, width, mult, tr_):
        outs = _sum_adam(name, parts_, _pack(ws, width, mult), _pack(ms, width, mult), _pack(vs, width, mult), tr_)
        shapes = [t.shape for t in ws]
        return [_unpack(o, shapes, width) for o in outs]

    big_m = [m_w_in, m_p_a, m_p_b, m_w_out, m_w_gate, m_w_up, m_w_down]
    big_v = [v_w_in, v_p_a, v_p_b, v_w_out, v_w_gate, v_w_up, v_w_down]
    gb, db_, mb, vb_ = adam_packed("adam_big", recv, big, big_m, big_v, d, LANE, LANE)

    gm, dm_, mm_, vm_ = adam_packed("adam_mod", g_wmod[None], [w_mod], [m_w_mod], [v_w_mod], nmod, 8, _tile(d, LANE))

    sm_w = [b_mod, norm1_w, q_norm_w, k_norm_w, lanes(a_log), lanes(dt_bias), o_norm_w, norm2_w, conv_w[0]]
    sm_m = [m_b_mod, m_norm1_w, m_q_norm_w, m_k_norm_w, lanes(m_a_log), lanes(m_dt_bias), m_o_norm_w, m_norm2_w, m_conv_w[0]]
    sm_v = [v_b_mod, v_norm1_w, v_q_norm_w, v_k_norm_w, lanes(v_a_log), lanes(v_dt_bias), v_o_norm_w, v_norm2_w, v_conv_w[0]]
    sm_g = [g_bmod, g_n1, g_qn, g_kn, g_al, g_dt, g_on, g_n2, g_conv_me]
    g_pack = _pack(sm_g, LANE, 8)
    gs_, ds_, ms_, vs_ = adam_packed("adam_small", g_pack[None], sm_w, sm_m, sm_v, LANE, 8, g_pack.shape[0])

    def assemble(big_l, mod_l, small_l):
        s_bmod, s_n1, s_qn, s_kn, s_al, s_dt, s_on, s_n2, s_conv = small_l
        b_in, b_pa, b_pb, b_out, b_gate, b_up, b_down = big_l
        return [mod_l[0], s_bmod, s_n1, b_in, s_qn, s_kn, s_conv[None], s_al[:, :heads], s_dt[:, :heads], s_on,
                b_pa, b_pb, b_out, s_n2, b_gate, b_up, b_down]

    outs = [loss, dx[None]]
    for big_l, mod_l, small_l in ((gb, gm, gs_), (db_, dm_, ds_), (mb, mm_, ms_), (vb_, vm_, vs_)):
        outs += assemble(big_l, mod_l, small_l)
    return tuple(outs)
```

```python
import jax
import jax.numpy as jnp
from jax import lax
from jax.experimental import pallas as pl
from jax.experimental.pallas import tpu as pltpu

F32, BF16 = jnp.float32, jnp.bfloat16
EPS = 1e-6
HEAD_DIM = 128
GDN_CHUNK = 64
GDN_GROUP = 256
CONV_TAPS = 4
N_DEV = 8
MESH_AXES = ("x", "y", "c")
ADAM_LR, ADAM_B1, ADAM_B2, ADAM_EPS, ADAM_WD, ADAM_STEP = 0.001, 0.9, 0.999, 1e-08, 0.01, 10
VMEM_LIMIT_BYTES = 56 * 1024 * 1024
LANE = 128
MESH_ID = pl.DeviceIdType.MESH


def _call(body, **kw):
    return pl.pallas_call(body, **kw)


def _params(*sem):
    return pltpu.CompilerParams(dimension_semantics=sem or None, vmem_limit_bytes=VMEM_LIMIT_BYTES)


def _tile(n, target):
    t = (min(n, target) // LANE) * LANE
    while t >= LANE:
        if n % t == 0:
            return t
        t -= LANE
    return n


def _silu(x):
    return x * jax.nn.sigmoid(x)


def _my_id():
    return 4 * lax.axis_index("x") + 2 * lax.axis_index("y") + lax.axis_index("c")


def _exchange(name, src, scatter):
    blk = src.shape[1:] if scatter else src.shape

    def body(src_ref, out_ref, send_sems, recv_sems, local_sem):
        x, y, c = lax.axis_index("x"), lax.axis_index("y"), lax.axis_index("c")
        me = 4 * x + 2 * y + c

        def peer(k):
            kx, ky, kc = (k >> 2) & 1, (k >> 1) & 1, k & 1
            px, py, pc = x ^ kx, y ^ ky, c ^ kc
            return (px, py, pc), 4 * px + 2 * py + pc

        def copy(k):
            dev, pid = peer(k)
            return pltpu.make_async_remote_copy(
                src_ref=src_ref.at[pid] if scatter else src_ref,
                dst_ref=out_ref.at[me],
                send_sem=send_sems.at[k - 1],
                recv_sem=recv_sems.at[k - 1],
                device_id=dev,
                device_id_type=MESH_ID,
            )

        def arrival(k):
            dev, pid = peer(k)
            return pltpu.make_async_remote_copy(
                src_ref=src_ref.at[pid] if scatter else src_ref,
                dst_ref=out_ref.at[pid],
                send_sem=send_sems.at[k - 1],
                recv_sem=recv_sems.at[k - 1],
                device_id=dev,
                device_id_type=MESH_ID,
            )

        mine = pltpu.make_async_copy(src_ref.at[me] if scatter else src_ref, out_ref.at[me], local_sem)
        mine.start()
        for k in range(1, N_DEV):
            copy(k).start()
        for k in range(1, N_DEV):
            arrival(k).wait_recv()
        for k in range(1, N_DEV):
            copy(k).wait_send()
        mine.wait()

    return _call(
        body,
        name=name,
        out_shape=jax.ShapeDtypeStruct((N_DEV,) + tuple(blk), src.dtype),
        in_specs=[pl.BlockSpec(memory_space=pl.ANY)],
        out_specs=pl.BlockSpec(memory_space=pl.ANY),
        scratch_shapes=[
            pltpu.SemaphoreType.DMA((N_DEV - 1,)),
            pltpu.SemaphoreType.DMA((N_DEV - 1,)),
            pltpu.SemaphoreType.DMA,
        ],
    )(src)


def _gather_two_level(name, src):
    def body(src_ref, out_ref, send_sems, recv_sems, local_sem):
        x, y, c = lax.axis_index("x"), lax.axis_index("y"), lax.axis_index("c")
        me, sibling = (x, y, c), (x, y, 1 - c)
        chips = [(1 - x, y), (x, 1 - y), (1 - x, 1 - y)]

        def slot(px, py, pc):
            return out_ref.at[4 * px + 2 * py + pc]

        def copy(k, block, to, from_src=False):
            return pltpu.make_async_remote_copy(
                src_ref=src_ref if from_src else slot(*block),
                dst_ref=slot(*block),
                send_sem=send_sems.at[k],
                recv_sem=recv_sems.at[k],
                device_id=to,
                device_id_type=MESH_ID,
            )

        mine = pltpu.make_async_copy(src_ref, slot(*me), local_sem)
        mine.start()
        first = [copy(0, me, sibling, True)] + [copy(1 + j, me, (*chip, c), True) for j, chip in enumerate(chips)]
        for cp in first:
            cp.start()
        passed = [copy(4 + j, (*chip, c), sibling) for j, chip in enumerate(chips)]
        for j, chip in enumerate(chips):
            copy(1 + j, (*chip, c), me).wait_recv()
            passed[j].start()
        copy(0, sibling, me).wait_recv()
        for j, chip in enumerate(chips):
            copy(4 + j, (*chip, 1 - c), me).wait_recv()
        for cp in first + passed:
            cp.wait_send()
        mine.wait()

    return _call(
        body,
        name=name,
        out_shape=jax.ShapeDtypeStruct((N_DEV,) + tuple(src.shape), src.dtype),
        in_specs=[pl.BlockSpec(memory_space=pl.ANY)],
        out_specs=pl.BlockSpec(memory_space=pl.ANY),
        scratch_shapes=[
            pltpu.SemaphoreType.DMA((N_DEV - 1,)),
            pltpu.SemaphoreType.DMA((N_DEV - 1,)),
            pltpu.SemaphoreType.DMA,
        ],
    )(src)


def _mm(name, a, b, *, ta=False, tb=False, out_dtype=F32, a_fn=None, tm=512, tn=512, tk=512):
    m, kdim = (a.shape[1], a.shape[0]) if ta else a.shape
    n = b.shape[0] if tb else b.shape[1]
    assert kdim == (b.shape[1] if tb else b.shape[0]), (a.shape, b.shape, ta, tb)
    tm, tn, tk = _tile(m, tm), _tile(n, tn), _tile(kdim, tk)
    nk = kdim // tk
    a_spec = pl.BlockSpec((tk, tm), lambda i, j, k: (k, i)) if ta else pl.BlockSpec((tm, tk), lambda i, j, k: (i, k))
    b_spec = pl.BlockSpec((tn, tk), lambda i, j, k: (j, k)) if tb else pl.BlockSpec((tk, tn), lambda i, j, k: (k, j))
    dims = (((0 if ta else 1,), (1 if tb else 0,)), ((), ()))

    def body(a_ref, b_ref, o_ref, acc_ref):
        k = pl.program_id(2)
        av = a_ref[...]
        if a_fn is not None:
            av = a_fn(av.astype(F32))
        p = lax.dot_general(av.astype(BF16), b_ref[...].astype(BF16), dims, preferred_element_type=F32)
        if nk == 1:
            o_ref[...] = p.astype(out_dtype)
        else:
            @pl.when(k == 0)
            def _():
                acc_ref[...] = p

            @pl.when(k > 0)
            def _():
                acc_ref[...] += p

            @pl.when(k == nk - 1)
            def _():
                o_ref[...] = acc_ref[...].astype(out_dtype)

    return _call(
        body,
        name=name,
        grid=(m // tm, n // tn, nk),
        in_specs=[a_spec, b_spec],
        out_specs=pl.BlockSpec((tm, tn), lambda i, j, k: (i, j)),
        out_shape=jax.ShapeDtypeStruct((m, n), out_dtype),
        scratch_shapes=[pltpu.VMEM((tm, tn) if nk > 1 else (8, LANE), F32)],
        compiler_params=_params("parallel", "parallel", "arbitrary"),
    )(a, b)


def _row(arr, w=None, off=0, stride=0):
    return (arr, "row", arr.shape[1] if w is None else w, off, stride)


def _par(arr, w=None, off=0, stride=0):
    return (arr, "par", arr.shape[1] if w is None else w, off, stride)


def _seg(name, fn, ins, outs, rows, tm, nh=1):
    nrow = rows // tm

    def spec(kind, w, off, stride):
        if kind == "row":
            return pl.BlockSpec((tm, w), lambda h, i: (i, off + stride * h))
        return pl.BlockSpec((1, w), lambda h, i: (0, off + stride * h))

    in_specs = [spec(kind, w, off, stride) for (_, kind, w, off, stride) in ins]
    out_specs = [spec("row" if kind == "row" else "par", w, off, stride) for (_, _, kind, w, off, stride) in outs]
    out_shape = [
        jax.ShapeDtypeStruct((rows if kind == "row" else 1, ncols), dt) for (ncols, dt, kind, _, _, _) in outs
    ]
    n_in = len(ins)

    def body(*refs):
        h, i = pl.program_id(0), pl.program_id(1)
        vals = fn(*[r[...] for r in refs[:n_in]])
        for (_, dt, kind, _, _, stride), ref, val in zip(outs, refs[n_in:], vals):
            if kind == "row":
                ref[...] = val.astype(dt)
            else:
                first = (i == 0) if stride != 0 else jnp.logical_and(i == 0, h == 0)

                @pl.when(first)
                def _(ref=ref, val=val):
                    ref[...] = val.astype(F32)

                @pl.when(jnp.logical_not(first))
                def _(ref=ref, val=val):
                    ref[...] += val.astype(F32)

    res = _call(
        body,
        name=name,
        grid=(nh, nrow),
        in_specs=in_specs,
        out_specs=out_specs,
        out_shape=out_shape,
        compiler_params=_params("arbitrary", "arbitrary"),
    )(*[t[0] for t in ins])
    return res


def _rmsn(x, w):
    return x * lax.rsqrt(jnp.mean(x * x, axis=-1, keepdims=True) + EPS) * w


def _f_mod(x, nw, sc, sh):
    return _rmsn(x, nw) * (1.0 + sc) + sh


def _f_res(x, t, g, nw, sc, sh):
    h = x + g * t
    return h, _f_mod(h, nw, sc, sh)


def _f_ff(gg, uu):
    return _silu(gg) * uu


def _f_merge(ya, yb, ga, gb):
    return jax.nn.sigmoid(ga) * ya + jax.nn.sigmoid(gb) * yb


def _f_qk(q, k, qw, kw):
    return _rmsn(q, qw), _rmsn(k, kw)


def _f_post(ob, z, ow):
    return _rmsn(ob, ow) * _silu(z)


def _f32(*xs):
    return [x.astype(F32) for x in xs]


def _log_sigmoid(z):
    return jnp.minimum(z, 0.0) - jnp.log(1.0 + jnp.exp(-jnp.abs(z)))


def _dot2(x, tri):
    hi = x.astype(BF16)
    lo = (x - hi.astype(F32)).astype(BF16)
    return jnp.dot(hi, tri, preferred_element_type=F32) + jnp.dot(lo, tri, preferred_element_type=F32)


def _tri(strict):
    j = lax.broadcasted_iota(jnp.int32, (LANE, LANE), 0)
    s = lax.broadcasted_iota(jnp.int32, (LANE, LANE), 1)
    return ((j > s) if strict else (j >= s)).astype(BF16)


SB_SLAB = 512
SB_UNROLL = 4


def _sb_slabs(bq, slab):
    return [slice(r * slab, (r + 1) * slab) for r in range(bq // slab)]


def _sb_diagonal(jj, n_slabs, slab):
    out = []
    for r in range(n_slabs):
        if jj * LANE >= (r + 1) * slab:
            continue
        out.append((r, "full" if (jj + 1) * LANE <= r * slab else "mask"))
    return out


def _sb_mask(r, jj, slab):
    t = r * slab + lax.broadcasted_iota(jnp.int32, (slab, LANE), 0)
    s = jj * LANE + lax.broadcasted_iota(jnp.int32, (slab, LANE), 1)
    return s < t


_NT = (((1,), (1,)), ((), ()))
_TN = (((0,), (0,)), ((), ()))


def _sb_fwd(qn, kn, vb, heads, bq):
    s_len = qn.shape[0]
    scale = HEAD_DIM ** -0.5
    sub = bq // LANE
    slab = min(SB_SLAB, bq)

    def body(q_ref, k_ref, v_ref, o_ref):
        i = pl.program_id(1)
        tri = _tri(True)
        slabs = _sb_slabs(bq, slab)
        qs = [q_ref[rw, :] for rw in slabs]

        def tile(q, k, v, cl, acc, mask):
            z = lax.dot_general(q, k, _NT, preferred_element_type=F32) * scale
            ls = _log_sigmoid(z)
            lm = ls - z
            if mask is not None:
                lm = jnp.where(mask, lm, 0.0)
            w = jnp.exp(ls + (_dot2(lm, tri) + cl))
            if mask is not None:
                w = jnp.where(mask, w, 0.0)
            return cl + jnp.sum(lm, axis=1, keepdims=True), acc + _dot2(w, v)

        state = [(jnp.zeros((slab, 1), F32), jnp.zeros((slab, LANE), F32)) for _ in slabs]
        for jj in reversed(range(sub)):
            j0 = pl.multiple_of(i * bq + jj * LANE, LANE)
            k, v = k_ref[pl.ds(j0, LANE), :], v_ref[pl.ds(j0, LANE), :]
            for r, kind in _sb_diagonal(jj, len(slabs), slab):
                state[r] = tile(qs[r], k, v, *state[r], _sb_mask(r, jj, slab) if kind == "mask" else None)

        def step(t, carry):
            for u in range(SB_UNROLL):
                j0 = pl.multiple_of((i * sub - 1 - (t * SB_UNROLL + u)) * LANE, LANE)
                k, v = k_ref[pl.ds(j0, LANE), :], v_ref[pl.ds(j0, LANE), :]
                carry = tuple(tile(qs[r], k, v, *carry[r], None) for r in range(len(slabs)))
            return carry

        state = lax.fori_loop(0, (i * sub) // SB_UNROLL, step, tuple(state))
        for r, rw in enumerate(slabs):
            o_ref[rw, :] = state[r][1]

    return _call(
        body,
        name="sb_fwd",
        grid=(heads, s_len // bq),
        in_specs=[
            pl.BlockSpec((bq, LANE), lambda h, i: (i, h)),
            pl.BlockSpec((s_len, LANE), lambda h, i: (0, h)),
            pl.BlockSpec((s_len, LANE), lambda h, i: (0, h)),
        ],
        out_specs=pl.BlockSpec((bq, LANE), lambda h, i: (i, h)),
        out_shape=jax.ShapeDtypeStruct(qn.shape, F32),
        compiler_params=_params("parallel", "arbitrary"),
    )(qn, kn, vb)


def _sb_bwd(qn, kn, vb, o, do, heads, bq):
    s_len = qn.shape[0]
    scale = HEAD_DIM ** -0.5
    sub = bq // LANE
    slab = min(SB_SLAB, bq)

    def body(q_ref, k_ref, v_ref, o_ref, do_ref, dq_ref, dk_ref, dv_ref):
        i = pl.program_id(1)

        @pl.when(i == 0)
        def _():
            dk_ref[...] = jnp.zeros_like(dk_ref)
            dv_ref[...] = jnp.zeros_like(dv_ref)

        tri, tri_inc = _tri(True), _tri(False)
        slabs = _sb_slabs(bq, slab)
        qs = [q_ref[rw, :] for rw in slabs]
        dobs = [do_ref[rw, :].astype(BF16) for rw in slabs]
        etots = [jnp.sum(dobs[r].astype(F32) * o_ref[rw, :], axis=1, keepdims=True) for r, rw in enumerate(slabs)]

        def tile(r, k, v, cl, ce, dq, mask):
            z = lax.dot_general(qs[r], k, _NT, preferred_element_type=F32) * scale
            ls = _log_sigmoid(z)
            lm = ls - z
            if mask is not None:
                lm = jnp.where(mask, lm, 0.0)
            a = jnp.exp(ls + (_dot2(lm, tri) + cl))
            if mask is not None:
                a = jnp.where(mask, a, 0.0)
            e = a * lax.dot_general(dobs[r], v, _NT, preferred_element_type=F32)
            before = etots[r] - (_dot2(e, tri_inc) + ce)
            sig = jnp.exp(ls)
            dz = (e * (1.0 - sig) - sig * before) * scale
            if mask is not None:
                dz = jnp.where(mask, dz, 0.0)
            dzb = dz.astype(BF16)
            dq = dq + jnp.dot(dzb, k, preferred_element_type=F32)
            carry = (cl + jnp.sum(lm, axis=1, keepdims=True), ce + jnp.sum(e, axis=1, keepdims=True), dq)
            return carry, dzb, a.astype(BF16)

        def scatter_kv(j0, dzbs, abs_, which):
            cat = lambda xs: xs[0] if len(xs) == 1 else jnp.concatenate(xs, axis=0)
            qcat, docat = cat([qs[r] for r in which]), cat([dobs[r] for r in which])
            dk_ref[pl.ds(j0, LANE), :] += lax.dot_general(cat(dzbs), qcat, _TN, preferred_element_type=F32)
            dv_ref[pl.ds(j0, LANE), :] += lax.dot_general(cat(abs_), docat, _TN, preferred_element_type=F32)

        zero = jnp.zeros((slab, 1), F32)
        state = [(zero, zero, jnp.zeros((slab, LANE), F32)) for _ in slabs]
        for jj in reversed(range(sub)):
            j0 = pl.multiple_of(i * bq + jj * LANE, LANE)
            k, v = k_ref[pl.ds(j0, LANE), :], v_ref[pl.ds(j0, LANE), :]
            dzbs, abs_, which = [], [], []
            for r, kind in _sb_diagonal(jj, len(slabs), slab):
                state[r], dzb, ab = tile(r, k, v, *state[r], _sb_mask(r, jj, slab) if kind == "mask" else None)
                dzbs.append(dzb)
                abs_.append(ab)
                which.append(r)
            scatter_kv(j0, dzbs, abs_, which)

        def step(t, carry):
            for u in range(SB_UNROLL):
                j0 = pl.multiple_of((i * sub - 1 - (t * SB_UNROLL + u)) * LANE, LANE)
                k, v = k_ref[pl.ds(j0, LANE), :], v_ref[pl.ds(j0, LANE), :]
                outs = [tile(r, k, v, *carry[r], None) for r in range(len(slabs))]
                scatter_kv(j0, [o[1] for o in outs], [o[2] for o in outs], list(range(len(slabs))))
                carry = tuple(o[0] for o in outs)
            return carry

        state = lax.fori_loop(0, (i * sub) // SB_UNROLL, step, tuple(state))
        for r, rw in enumerate(slabs):
            dq_ref[rw, :] = state[r][2]

    tile_spec = pl.BlockSpec((bq, LANE), lambda h, i: (i, h))
    full = pl.BlockSpec((s_len, LANE), lambda h, i: (0, h))
    shp = jax.ShapeDtypeStruct(qn.shape, F32)
    return _call(
        body,
        name="sb_bwd",
        grid=(heads, s_len // bq),
        in_specs=[tile_spec, full, full, tile_spec, tile_spec],
        out_specs=[tile_spec, full, full],
        out_shape=[shp, shp, shp],
        compiler_params=_params("parallel", "arbitrary"),
    )(qn, kn, vb, o, do)


def _shift_down(cur, halo, k):
    if k == 0:
        return cur
    r = pltpu.roll(cur, k, 0)
    p = pltpu.roll(halo, k, 0)
    top = jnp.where(lax.broadcasted_iota(jnp.int32, halo.shape, 0) < k, p, r[:8])
    return jnp.concatenate([top, r[8:]], axis=0)


def _shift_up(cur, halo, k):
    if k == 0:
        return cur
    n = cur.shape[0]
    r = pltpu.roll(cur, n - k, 0)
    p = pltpu.roll(halo, 8 - k, 0)
    bot = jnp.where(lax.broadcasted_iota(jnp.int32, halo.shape, 0) >= 8 - k, p, r[n - 8:])
    return jnp.concatenate([r[: n - 8], bot], axis=0)


def _f_qkv_act(pre, jb, heads):
    act = _silu(pre)
    nrm = act * lax.rsqrt(jnp.sum(act * act, axis=-1, keepdims=True) + EPS)
    nrm = nrm * jnp.where(jb < heads, HEAD_DIM ** -0.5, 1.0)
    return jnp.where(jb < 2 * heads, nrm, act)


def _taps(w_ref):
    return [w_ref[j:j + 1, :] for j in range(CONV_TAPS)]


def _conv_pre(x_ref, w, r0, tr):
    cur = x_ref[pl.ds(r0, tr), :]
    halo = x_ref[pl.ds(jnp.maximum(r0 - 8, 0), 8), :]
    halo = jnp.where(r0 > 0, halo, 0.0)
    shifted = [_shift_down(cur, halo, CONV_TAPS - 1 - j) for j in range(CONV_TAPS)]
    pre = sum(w[j] * shifted[j] for j in range(CONV_TAPS))
    return pre, shifted


def _bprep_fwd(proj, col0, conv_w8, heads, tr):
    s_len = proj.shape[0]
    ncol = 3 * heads
    cb0 = col0 // LANE

    def body(x_ref, w_ref, o_ref):
        jb = pl.program_id(0)
        w = _taps(w_ref)

        def step(t, _):
            r0 = pl.multiple_of(t * tr, tr)
            pre, _unused = _conv_pre(x_ref, w, r0, tr)
            o_ref[pl.ds(r0, tr), :] = _f_qkv_act(pre, jb, heads)
            return 0

        lax.fori_loop(0, s_len // tr, step, 0)

    return _call(
        body,
        name="bprep_fwd",
        grid=(ncol,),
        in_specs=[
            pl.BlockSpec((s_len, LANE), lambda j: (0, cb0 + j)),
            pl.BlockSpec((8, LANE), lambda j: (0, j)),
        ],
        out_specs=pl.BlockSpec((s_len, LANE), lambda j: (0, j)),
        out_shape=jax.ShapeDtypeStruct((s_len, ncol * LANE), F32),
        compiler_params=_params("parallel"),
    )(proj, conv_w8)


def _bprep_bwd(proj, col0, conv_w8, dqkv, heads, tr):
    s_len = proj.shape[0]
    ncol = 3 * heads
    cb0 = col0 // LANE
    nt = s_len // tr

    def body(x_ref, w_ref, d_ref, dx_ref, dw_ref, dpre_ref):
        jb = pl.program_id(0)
        w = _taps(w_ref)

        def pass1(t, dws):
            r0 = pl.multiple_of(t * tr, tr)
            pre, shifted = _conv_pre(x_ref, w, r0, tr)
            _, vjp = jax.vjp(lambda p: _f_qkv_act(p, jb, heads), pre)
            (dpre,) = vjp(d_ref[pl.ds(r0, tr), :])
            dpre_ref[pl.ds(r0, tr), :] = dpre
            return tuple(dws[j] + jnp.sum(dpre * shifted[j], axis=0, keepdims=True) for j in range(CONV_TAPS))

        dws = lax.fori_loop(0, nt, pass1, tuple(jnp.zeros((1, LANE), F32) for _ in range(CONV_TAPS)))
        for j in range(CONV_TAPS):
            dw_ref[j:j + 1, :] = dws[j]
        dw_ref[CONV_TAPS:, :] = jnp.zeros((8 - CONV_TAPS, LANE), F32)

        def pass2(t, _):
            r0 = pl.multiple_of(t * tr, tr)
            cur = dpre_ref[pl.ds(r0, tr), :]
            halo = dpre_ref[pl.ds(jnp.minimum(r0 + tr, s_len - 8), 8), :]
            halo = jnp.where(r0 + tr < s_len, halo, 0.0)
            dx = sum(w[j] * _shift_up(cur, halo, CONV_TAPS - 1 - j) for j in range(CONV_TAPS))
            dx_ref[pl.ds(r0, tr), :] = dx.astype(BF16)
            return 0

        lax.fori_loop(0, nt, pass2, 0)

    col = pl.BlockSpec((s_len, LANE), lambda j: (0, j))
    w_spec = pl.BlockSpec((8, LANE), lambda j: (0, j))
    return _call(
        body,
        name="bprep_bwd",
        grid=(ncol,),
        in_specs=[pl.BlockSpec((s_len, LANE), lambda j: (0, cb0 + j)), w_spec, col],
        out_specs=[col, w_spec],
        out_shape=[jax.ShapeDtypeStruct((s_len, ncol * LANE), BF16), jax.ShapeDtypeStruct((8, ncol * LANE), F32)],
        scratch_shapes=[pltpu.VMEM((s_len, LANE), F32)],
        compiler_params=_params("parallel"),
    )(proj, conv_w8, dqkv)


def _f_gbeta(bt, at, a_log, dt_bias):
    xs = at + dt_bias
    softplus = jnp.maximum(xs, 0.0) + jnp.log(1.0 + jnp.exp(-jnp.abs(xs)))
    return jax.nn.sigmoid(bt), -jnp.exp(a_log) * softplus


def _small(name, fn, ins, out_shapes):
    def body(*refs):
        vals = fn(*[r[...] for r in refs[:len(ins)]])
        for ref, val in zip(refs[len(ins):], vals):
            ref[...] = val.astype(ref.dtype)

    return _call(body, name=name, out_shape=[jax.ShapeDtypeStruct(s, F32) for s in out_shapes],
                 compiler_params=_params())(*ins)


def _dot3(a, b, dims=(((1,), (0,)), ((), ()))):
    ah, bh = a.astype(BF16), b.astype(BF16)
    al, bl = (a - ah.astype(F32)).astype(BF16), (b - bh.astype(F32)).astype(BF16)
    d = lambda u, v: lax.dot_general(u, v, dims, preferred_element_type=F32)
    return d(ah, bh) + (d(ah, bl) + d(al, bh))


def _bdot(a, b, dims=(((1,), (0,)), ((), ()))):
    return lax.dot_general(a.astype(BF16), b.astype(BF16), dims, preferred_element_type=F32)


@jax.custom_vjp
def _unit_lower_inverse(nmat):
    r = lax.broadcasted_iota(jnp.int32, nmat.shape, 0)
    i = lax.broadcasted_iota(jnp.int32, nmat.shape, 1)
    am = -nmat
    tinv = jnp.where(r == i, 1.0, 0.0) + am
    for _ in range(GDN_CHUNK.bit_length() - 2):
        am = _dot3(am, am)
        tinv = tinv + _dot3(tinv, am)
    return tinv


def _unit_lower_inverse_fwd(nmat):
    tinv = _unit_lower_inverse(nmat)
    return tinv, tinv


def _unit_lower_inverse_bwd(tinv, dt):
    return (-_dot3(_dot3(tinv, dt, _TN), tinv, _NT),)


_unit_lower_inverse.defvjp(_unit_lower_inverse_fwd, _unit_lower_inverse_bwd)


def _gdn_group(s0, q, k, v, g_row, b_row):
    gs = q.shape[0]
    r = lax.broadcasted_iota(jnp.int32, (gs, gs), 0)
    i = lax.broadcasted_iota(jnp.int32, (gs, gs), 1)
    same = (r // GDN_CHUNK) == (i // GDN_CHUNK)
    g_b = jnp.broadcast_to(g_row, (gs, gs))
    b_b = jnp.broadcast_to(b_row, (gs, gs))
    eye = r == i
    g_col = jnp.sum(jnp.where(eye, g_b, 0.0), axis=1, keepdims=True)
    b_col = jnp.sum(jnp.where(eye, b_b, 0.0), axis=1, keepdims=True)
    gc_col = jnp.sum(jnp.where(same & (i <= r), g_b, 0.0), axis=1, keepdims=True)
    gc_row = jnp.sum(jnp.where(same & (r <= i), jnp.broadcast_to(g_col, (gs, gs)), 0.0), axis=0, keepdims=True)
    gl_col = jnp.sum(jnp.where(same, g_b, 0.0), axis=1, keepdims=True)
    tril = same & (i <= r)
    decay = jnp.where(tril, jnp.exp(jnp.where(tril, gc_col - gc_row, 0.0)), 0.0)
    kk = _bdot(k, k, _NT)
    nmat = jnp.where(same & (i < r), b_col * kk * decay, 0.0)
    tinv = _unit_lower_inverse(nmat)
    eg = jnp.exp(gc_col)
    w_v = _dot3(tinv, b_col * v)
    w_k = _dot3(tinv, (b_col * eg) * k)
    attn = _bdot(q, k, _NT) * decay
    q_g = q * eg
    k_dec = k * jnp.exp(gl_col - gc_col)
    ridx = lax.broadcasted_iota(jnp.int32, (gs, 1), 0)
    s = s0
    u_all = jnp.zeros_like(v)
    o_inter = jnp.zeros_like(v)
    for c in range(gs // GDN_CHUNK):
        in_chunk = (ridx // GDN_CHUNK) == c
        u = jnp.where(in_chunk, w_v - _bdot(w_k, s), 0.0)
        o_inter = o_inter + jnp.where(in_chunk, _bdot(q_g, s), 0.0)
        u_all = u_all + u
        gl = jnp.sum(jnp.where(ridx == c * GDN_CHUNK, gl_col, 0.0), axis=0, keepdims=True)
        s = jnp.exp(gl) * s + _bdot(k_dec, u, _TN)
    return o_inter + _bdot(attn, u_all), s


GDN_PAIR = 2


def _gdn_specs(heads, ngrp, rev):
    blk = (lambda n: ngrp - 1 - n) if rev else (lambda n: n)
    width = GDN_PAIR * LANE
    def col(off):
        return pl.BlockSpec((GDN_GROUP, width), lambda h, n: (blk(n), (off * heads) // GDN_PAIR + h))
    vec = pl.BlockSpec((GDN_PAIR, ngrp, GDN_GROUP), lambda h, n: (h, 0, 0))
    state = pl.BlockSpec((GDN_PAIR, 1, HEAD_DIM, HEAD_DIM), lambda h, n: (h, blk(n), 0, 0))
    out_col = pl.BlockSpec((GDN_GROUP, width), lambda h, n: (blk(n), h))
    return col, vec, state, out_col


def _gdn_fwd(qkv, g3, b3, heads):
    s_len = qkv.shape[0]
    ngrp = s_len // GDN_GROUP
    col, vec, state, out_col = _gdn_specs(heads, ngrp, False)

    def body(q_ref, k_ref, v_ref, g_ref, b_ref, o_ref, st_ref, s_scr):
        n = pl.program_id(1)

        @pl.when(n == 0)
        def _():
            s_scr[...] = jnp.zeros_like(s_scr)

        for a in range(GDN_PAIR):
            ln = slice(a * LANE, (a + 1) * LANE)
            s0 = s_scr[a]
            st_ref[a, 0] = s0
            o, s1 = _gdn_group(s0, q_ref[:, ln], k_ref[:, ln], v_ref[:, ln],
                               g_ref[a, pl.ds(n, 1), :], b_ref[a, pl.ds(n, 1), :])
            o_ref[:, ln] = o
            s_scr[a] = s1

    return _call(
        body,
        name="gdn_fwd",
        grid=(heads // GDN_PAIR, ngrp),
        in_specs=[col(0), col(1), col(2), vec, vec],
        out_specs=[out_col, state],
        out_shape=[
            jax.ShapeDtypeStruct((s_len, heads * LANE), F32),
            jax.ShapeDtypeStruct((heads, ngrp, HEAD_DIM, HEAD_DIM), F32),
        ],
        scratch_shapes=[pltpu.VMEM((GDN_PAIR, HEAD_DIM, HEAD_DIM), F32)],
        compiler_params=_params("parallel", "arbitrary"),
    )(qkv, qkv, qkv, g3, b3)


def _gdn_bwd(qkv, g3, b3, states, do, heads):
    s_len = qkv.shape[0]
    ngrp = s_len // GDN_GROUP
    col, vec, state, out_col = _gdn_specs(heads, ngrp, True)

    def body(q_ref, k_ref, v_ref, g_ref, b_ref, st_ref, do_ref, dq_ref, dk_ref, dv_ref, dg_ref, db_ref, ds_scr):
        n = pl.program_id(1)
        grp = ngrp - 1 - n

        @pl.when(n == 0)
        def _():
            ds_scr[...] = jnp.zeros_like(ds_scr)

        for a in range(GDN_PAIR):
            ln = slice(a * LANE, (a + 1) * LANE)
            _, vjp = jax.vjp(_gdn_group, st_ref[a, 0], q_ref[:, ln], k_ref[:, ln], v_ref[:, ln],
                             g_ref[a, pl.ds(grp, 1), :], b_ref[a, pl.ds(grp, 1), :])
            ds0, dq, dk, dv, dg, db = vjp((do_ref[:, ln], ds_scr[a]))
            dq_ref[:, ln] = dq
            dk_ref[:, ln] = dk
            dv_ref[:, ln] = dv
            dg_ref[a, pl.ds(grp, 1), :] = dg
            db_ref[a, pl.ds(grp, 1), :] = db
            ds_scr[a] = ds0

    shp = jax.ShapeDtypeStruct((s_len, heads * LANE), F32)
    vshp = jax.ShapeDtypeStruct(g3.shape, F32)
    return _call(
        body,
        name="gdn_bwd",
        grid=(heads // GDN_PAIR, ngrp),
        in_specs=[col(0), col(1), col(2), vec, vec, state, out_col],
        out_specs=[out_col, out_col, out_col, vec, vec],
        out_shape=[shp, shp, shp, vshp, vshp],
        scratch_shapes=[pltpu.VMEM((GDN_PAIR, HEAD_DIM, HEAD_DIM), F32)],
        compiler_params=_params("parallel", "arbitrary"),
    )(qkv, qkv, qkv, g3, b3, states, do)


def _sum_adam(name, parts, w, m, v, tr):
    n, rows, cols = parts.shape

    def body(p_ref, w_ref, m_ref, v_ref, g_out, d_out, m_out, v_out):
        g = p_ref[0].astype(F32)
        for s in range(1, n):
            g = g + p_ref[s].astype(F32)
        m_new = ADAM_B1 * m_ref[...] + (1.0 - ADAM_B1) * g
        v_new = ADAM_B2 * v_ref[...] + (1.0 - ADAM_B2) * (g * g)
        m_hat = m_new / (1.0 - ADAM_B1 ** ADAM_STEP)
        v_hat = v_new / (1.0 - ADAM_B2 ** ADAM_STEP)
        g_out[...] = g
        d_out[...] = -ADAM_LR * (m_hat / (jnp.sqrt(v_hat) + ADAM_EPS) + ADAM_WD * w_ref[...])
        m_out[...] = m_new
        v_out[...] = v_new

    mat = pl.BlockSpec((tr, cols), lambda i: (i, 0))
    shp = jax.ShapeDtypeStruct((rows, cols), F32)
    return _call(
        body,
        name=name,
        grid=(rows // tr,),
        in_specs=[pl.BlockSpec((n, tr, cols), lambda i: (0, i, 0)), mat, mat, mat],
        out_specs=[mat, mat, mat, mat],
        out_shape=[shp, shp, shp, shp],
        compiler_params=_params("parallel"),
    )(parts, w, m, v)


def _sum_parts(name, parts):
    n = parts.shape[0]

    def fn(p):
        g = p[0]
        for s in range(1, n):
            g = g + p[s]
        return (g,)

    return _small(name, fn, [parts], [parts.shape[1:]])[0]


def _pad_rows(a, mult):
    r = (-a.shape[0]) % mult
    return a if r == 0 else jnp.concatenate([a, jnp.zeros((r,) + a.shape[1:], a.dtype)], axis=0)


def _pad_lanes(a, width):
    return jnp.concatenate([a, jnp.zeros(a.shape[:-1] + (width - a.shape[-1],), a.dtype)], axis=-1)


def _pack(pieces, width, mult):
    return _pad_rows(jnp.concatenate([p.reshape(-1, width) for p in pieces], axis=0), mult)


def _unpack(packed, shapes, width):
    out, r0 = [], 0
    for shp in shapes:
        size = 1
        for d in shp:
            size *= d
        nr = size // width
        out.append(packed[..., r0:r0 + nr, :].reshape(packed.shape[:-2] + tuple(shp)))
        r0 += nr
    return out


def kernel(x, c, w_mod, b_mod, norm1_w, w_in, q_norm_w, k_norm_w, conv_w, a_log, dt_bias, o_norm_w, p_a, p_b, w_out, norm2_w, w_gate, w_up, w_down, loss_target, m_w_mod, m_b_mod, m_norm1_w, m_w_in, m_q_norm_w, m_k_norm_w, m_conv_w, m_a_log, m_dt_bias, m_o_norm_w, m_p_a, m_p_b, m_w_out, m_norm2_w, m_w_gate, m_w_up, m_w_down, v_w_mod, v_b_mod, v_norm1_w, v_w_in, v_q_norm_w, v_k_norm_w, v_conv_w, v_a_log, v_dt_bias, v_o_norm_w, v_p_a, v_p_b, v_w_out, v_norm2_w, v_w_gate, v_w_up, v_w_down):
    s_len, d = x.shape[1], x.shape[2]
    heads = a_log.shape[1]
    dh = heads * HEAD_DIM
    f = w_down.shape[1] * N_DEV
    din_loc = w_in.shape[2]
    me = _my_id()
    x2, tgt = x[0], loss_target[0]

    row_sharded = [p_a, p_b, w_out, w_down]
    wg = _gather_two_level("gather_rows", _pack([t[0].astype(BF16) for t in row_sharded], d, LANE))
    g_pa, g_pb, g_out, g_down = _unpack(wg, [t.shape[1:] for t in row_sharded], d)
    g_in = _gather_two_level("gather_w_in", w_in[0].astype(BF16))
    g_gu = _gather_two_level("gather_w_gu", jnp.stack([w_gate[0], w_up[0]]).astype(BF16))
    g_gate, g_up = g_gu[:, 0], g_gu[:, 1]
    cols = lambda t: t.transpose(1, 0, 2).reshape(t.shape[1], -1)
    rows = lambda t: t.reshape(-1, t.shape[2])
    w_in_g = cols(g_in)
    o_ba = 3 * dh + 3 * dh + dh
    din = w_in_g.shape[1]
    n_perm = o_ba + 2 * d + LANE
    w_in_p = jnp.concatenate(
        [w_in_g[:, :o_ba], w_in_g[:, o_ba + 2 * heads:], w_in_g[:, o_ba:o_ba + 2 * heads],
         jnp.zeros((d, LANE - 2 * heads), BF16)], axis=1)
    o_qkvb, o_z, o_ga, o_bad = 3 * dh, 6 * dh, 7 * dh, 7 * dh + 2 * d
    w_pa, w_pb, w_o, w_dn = rows(g_pa), rows(g_pb), rows(g_out), rows(g_down)
    w_gu = jnp.concatenate([cols(g_gate), cols(g_up)], axis=1)

    c_all = _pad_rows(_exchange("gather_c", c, False).reshape(N_DEV, d), LANE)
    mod_part = _mm("mod_fwd", c_all, w_mod[0], a_fn=_silu, tk=d)[:N_DEV]
    mod_all = _exchange("gather_mod", mod_part, False)
    mod_me = lax.dynamic_index_in_dim(mod_all, me, axis=1, keepdims=False).reshape(1, 6 * d) + b_mod
    sh1, sc1, gt1, sh2, sc2, gt2 = [mod_me[:, j * d:(j + 1) * d] for j in range(6)]

    tm = min(128, s_len)
    th = min(1024, s_len)
    (u1,) = _seg("pre1_fwd", lambda a, nw, sc, sh: (_f_mod(a, nw, sc, sh),),
                 [_row(x2), _par(norm1_w), _par(sc1), _par(sh1)], [(d, BF16, "row", d, 0, 0)], s_len, tm)
    proj = _mm("proj", u1, w_in_p, tm=1024, tn=640, tk=d)

    hb = dh // LANE
    qn, kn, vb = _seg(
        "qk_fwd", lambda q, k, v, qw, kw: _f_qk(q, k, qw, kw) + (v,),
        [_row(proj, LANE, 0, 1), _row(proj, LANE, hb, 1), _row(proj, LANE, 2 * hb, 1), _par(q_norm_w), _par(k_norm_w)],
        [(dh, BF16, "row", LANE, 0, 1)] * 3, s_len, th, nh=hb)
    bq = min(512, s_len)
    o_a = _sb_fwd(qn, kn, vb, heads, bq)

    conv_all = _exchange("gather_conv", conv_w[0], False)
    conv_g = _pad_rows(conv_all.transpose(1, 0, 2).reshape(CONV_TAPS, 3 * dh), 8)
    tr = min(512, s_len)
    qkv_b = _bprep_fwd(proj, o_qkvb, conv_g, heads, tr)
    ba_t = proj[:, o_bad:o_bad + 2 * heads].T
    a_col, dt_col = a_log.reshape(heads, 1), dt_bias.reshape(heads, 1)
    beta_t, g_t = _small("gbeta_fwd", _f_gbeta, [ba_t[:heads], ba_t[heads:], a_col, dt_col], [(heads, s_len)] * 2)
    ngrp = s_len // GDN_GROUP
    g3, b3 = g_t.reshape(heads, ngrp, GDN_GROUP), beta_t.reshape(heads, ngrp, GDN_GROUP)
    o_b, states = _gdn_fwd(qkv_b, g3, b3, heads)
    zoff = o_z // LANE
    (ob2,) = _seg("post_fwd", lambda ob, z, ow: (_f_post(ob, z, ow),),
                  [_row(o_b, LANE, 0, 1), _row(proj, LANE, zoff, 1), _par(o_norm_w)],
                  [(dh, BF16, "row", LANE, 0, 1)], s_len, th, nh=hb)

    ya = _mm("ya", o_a, w_pa, tm=1024, tk=dh)
    yb = _mm("yb", ob2, w_pb, tm=1024, tk=dh)
    goff = o_ga // d
    gate_ins = [_row(proj, d, goff, 0), _row(proj, d, goff + 1, 0)]
    (merged,) = _seg("merge_fwd", lambda a, b, ga, gb: (_f_merge(a, b, ga, gb),),
                     [_row(ya), _row(yb)] + gate_ins, [(d, BF16, "row", d, 0, 0)], s_len, tm)
    t_out = _mm("attn_out", merged, w_o, tm=1024, tk=d)
    res_par = [_par(gt1), _par(norm2_w), _par(sc2), _par(sh2)]
    h1, u2 = _seg("res1_fwd", _f_res, [_row(x2), _row(t_out)] + res_par,
                  [(d, F32, "row", d, 0, 0), (d, BF16, "row", d, 0, 0)], s_len, tm)
    gu = _mm("ffn_in", u2, w_gu, tm=1024, tk=d)
    tf = 128
    (ff,) = _seg("ff_fwd", lambda a, b: (_f_ff(a, b),), [_row(gu, f, 0, 0), _row(gu, f, 1, 0)],
                 [(f, BF16, "row", f, 0, 0)], s_len, tf)
    dn = _mm("ffn_out", ff, w_dn, tm=1024, tk=f // 2)

    def f_loss(hh, dd, g2, tg):
        err = hh + g2 * dd - tg
        dh2 = err * (1.0 / d)
        return (dh2, g2 * dh2, jnp.sum(err * err, axis=0, keepdims=True), jnp.sum(dh2 * dd, axis=0, keepdims=True))

    dh2, ddn, lsq, dgt2 = _seg(
        "loss", f_loss, [_row(h1), _row(dn), _par(gt2), _row(tgt)],
        [(d, F32, "row", d, 0, 0), (d, BF16, "row", d, 0, 0), (d, F32, "acc", d, 0, 0), (d, F32, "acc", d, 0, 0)],
        s_len, tm)
    loss = lax.psum(0.5 * jnp.sum(lsq) / d, MESH_AXES)

    dff = _mm("d_ff", ddn, w_dn, tb=True, tm=1024, tk=d)
    dw_down = _mm("dw_down", ff, ddn, ta=True, out_dtype=BF16, tk=2048)

    def f_ff_bwd(a, b, dy):
        _, vjp = jax.vjp(_f_ff, a, b)
        da, db = vjp(dy)
        return (jnp.concatenate([da, db], axis=1),)

    (dgu,) = _seg("ff_bwd", f_ff_bwd, [_row(gu, f, 0, 0), _row(gu, f, 1, 0), _row(dff)],
                  [(2 * f, BF16, "row", 2 * f, 0, 0)], s_len, tf)
    du2 = _mm("d_u2", dgu, w_gu, tb=True, tm=1024, tk=f // 2)
    dw_gu = _mm("dw_gu", u2, dgu, ta=True, out_dtype=BF16, tm=1024, tk=2048)

    def f_res_bwd(a, t, g, nw, sc, sh, dhd, du):
        _, vjp = jax.vjp(_f_res, a, t, g, nw, sc, sh)
        da, dt_, dg, dnw, dsc, dsh = vjp((dhd, du))
        return da, dt_, dg, dnw, dsc, dsh

    acc_d = (d, F32, "acc", d, 0, 0)
    dh1, dt_out, dgt1, dnorm2, dsc2, dsh2 = _seg(
        "res1_bwd", f_res_bwd, [_row(x2), _row(t_out)] + res_par + [_row(dh2), _row(du2)],
        [(d, F32, "row", d, 0, 0), (d, BF16, "row", d, 0, 0), acc_d, acc_d, acc_d, acc_d], s_len, tm)

    dmerged = _mm("d_merged", dt_out, w_o, tb=True, tm=1024, tk=d)
    dw_out = _mm("dw_out", merged, dt_out, ta=True, out_dtype=BF16, tm=1024, tk=2048)

    def f_merge_bwd(a, b, ga, gb, dy):
        _, vjp = jax.vjp(_f_merge, a, b, ga, gb)
        da, db, dga, dgb = vjp(dy)
        return da, db, jnp.concatenate([dga, dgb], axis=1)

    dya, dyb, dgates = _seg("merge_bwd", f_merge_bwd, [_row(ya), _row(yb)] + gate_ins + [_row(dmerged)],
                            [(d, BF16, "row", d, 0, 0), (d, BF16, "row", d, 0, 0), (2 * d, BF16, "row", 2 * d, 0, 0)],
                            s_len, tm)
    do_a = _mm("d_oa", dya, w_pa, tb=True, tm=1024, tk=d)
    dw_pa = _mm("dw_pa", o_a, dya, ta=True, out_dtype=BF16, tk=2048)
    dob2 = _mm("d_ob2", dyb, w_pb, tb=True, tm=1024, tk=d)
    dw_pb = _mm("dw_pb", ob2, dyb, ta=True, out_dtype=BF16, tm=1024, tk=2048)

    def f_post_bwd(ob, z, ow, dy):
        _, vjp = jax.vjp(_f_post, ob, z, ow)
        return vjp(dy)

    acc_h = (LANE, F32, "acc", LANE, 0, 0)
    d_ob, dz, d_onorm = _seg(
        "post_bwd", f_post_bwd,
        [_row(o_b, LANE, 0, 1), _row(proj, LANE, zoff, 1), _par(o_norm_w), _row(dob2, LANE, 0, 1)],
        [(dh, F32, "row", LANE, 0, 1), (dh, BF16, "row", LANE, 0, 1), acc_h], s_len, th, nh=hb)
    dqb, dkb, dvb, dg3, db3 = _gdn_bwd(qkv_b, g3, b3, states, d_ob, heads)
    dqkv_n = jnp.concatenate([dqb, dkb, dvb], axis=1)
    dqkv_pre, dconv8 = _bprep_bwd(proj, o_qkvb, conv_g, dqkv_n, heads, tr)

    def f_gbeta_bwd(bt, at, al, dtb, dbeta, dg):
        _, vjp = jax.vjp(_f_gbeta, bt, at, al, dtb)
        return vjp((dbeta, dg))

    dbt, dat, da_log, ddt = _small(
        "gbeta_bwd", f_gbeta_bwd,
        [ba_t[:heads], ba_t[heads:], a_col, dt_col, db3.reshape(heads, s_len), dg3.reshape(heads, s_len)],
        [(heads, s_len), (heads, s_len), (heads, 1), (heads, 1)])
    dba = _pad_lanes(jnp.concatenate([dbt, dat], axis=0).T, LANE).astype(BF16)

    dqn, dkn, dva = _sb_bwd(qn, kn, vb, o_a, do_a, heads, bq)

    def f_qk_bwd(q, k, qw, kw, dq, dk):
        _, vjp = jax.vjp(_f_qk, q, k, qw, kw)
        return vjp((dq, dk))

    dqa, dka, d_qnorm, d_knorm = _seg(
        "qk_bwd", f_qk_bwd,
        [_row(proj, LANE, 0, 1), _row(proj, LANE, hb, 1), _par(q_norm_w), _par(k_norm_w),
         _row(dqn, LANE, 0, 1), _row(dkn, LANE, 0, 1)],
        [(dh, BF16, "row", LANE, 0, 1), (dh, BF16, "row", LANE, 0, 1), acc_h, acc_h], s_len, th, nh=hb)

    dproj = jnp.concatenate([dqa, dka, dva.astype(BF16), dqkv_pre, dz, dgates, dba], axis=1)
    du1 = _mm("d_u1", dproj, w_in_p, tb=True, tm=1024, tk=n_perm // 5)
    dw_in_p = _mm("dw_in", u1, dproj, ta=True, out_dtype=BF16, tm=1024, tn=640, tk=2048)

    def f_pre_bwd(a, nw, sc, sh, du, dres):
        _, vjp = jax.vjp(_f_mod, a, nw, sc, sh)
        da, dnw, dsc, dsh = vjp(du)
        return da + dres, dnw, dsc, dsh

    dx, dnorm1, dsc1, dsh1 = _seg(
        "pre1_bwd", f_pre_bwd, [_row(x2), _par(norm1_w), _par(sc1), _par(sh1), _row(du1), _row(dh1)],
        [(d, F32, "row", d, 0, 0), acc_d, acc_d, acc_d], s_len, tm)

    dmod = jnp.concatenate([dsh1, dsc1, dgt1, dsh2, dsc2, dgt2], axis=1)
    lanes = lambda t: _pad_lanes(t.reshape(1, -1), LANE)
    small = [dmod, dnorm1, d_qnorm, d_knorm, lanes(da_log), lanes(ddt), d_onorm, dnorm2, dconv8[:CONV_TAPS]]
    small_shapes = [t.shape for t in small]
    small_all = _exchange("gather_small", _pack(small, LANE, 8), False)
    small_sum = _sum_parts("sum_small", small_all)
    g_bmod, g_n1, g_qn, g_kn, g_al, g_dt, g_on, g_n2, g_conv = _unpack(small_sum, small_shapes, LANE)
    ncv = conv_w.shape[2]
    g_conv_me = lax.dynamic_slice_in_dim(g_conv, me * ncv, ncv, axis=1)

    nmod = w_mod.shape[2]
    dmod_all = _unpack(small_all, small_shapes[:1], LANE)[0].reshape(N_DEV, 6 * d)
    dmod_cols = _pad_rows(lax.dynamic_slice_in_dim(dmod_all, me * nmod, nmod, axis=1), LANE)
    g_wmod = _mm("dw_mod", c_all, dmod_cols, ta=True, a_fn=_silu, tk=LANE)

    dw_in_g = jnp.concatenate([dw_in_p[:, :o_ba], dw_in_p[:, o_bad:o_bad + 2 * heads], dw_in_p[:, o_ga:o_ga + 2 * d]], axis=1)
    to_cols = lambda t: t.reshape(t.shape[0], N_DEV, -1).transpose(1, 0, 2)
    to_rows = lambda t: t.reshape(N_DEV, -1, t.shape[1])
    send = jnp.concatenate([to_rows(dw_pa), to_rows(dw_pb), to_rows(dw_out), to_rows(dw_down)], axis=1)
    pad = (-send.shape[1]) % LANE
    if pad:
        send = jnp.concatenate([send, jnp.zeros((N_DEV, pad, d), BF16)], axis=1)
    recv_rows = _exchange("scatter_rows", send, True)
    recv_in = _exchange("scatter_w_in", to_cols(dw_in_g), True)
    recv_gu = _exchange("scatter_w_gu", jnp.concatenate([to_cols(dw_gu[:, :f]), to_cols(dw_gu[:, f:])], axis=1), True)

    def adam_packed(name, parts_, ws, ms, vs, width, mult, tr_):
        outs = _sum_adam(name, parts_, _pack(ws, width, mult), _pack(ms, width, mult), _pack(vs, width, mult), tr_)
        shapes = [t.shape for t in ws]
        return [_unpack(o, shapes, width) for o in outs]

    rows_l = adam_packed("adam_rows", recv_rows, row_sharded, [m_p_a, m_p_b, m_w_out, m_w_down],
                         [v_p_a, v_p_b, v_w_out, v_w_down], d, LANE, LANE)
    in_l = adam_packed("adam_w_in", recv_in, [w_in], [m_w_in], [v_w_in], din_loc, 8, _tile(d, LANE) // 2)
    gu_l = adam_packed("adam_w_gu", recv_gu, [w_gate, w_up], [m_w_gate, m_w_up], [v_w_gate, v_w_up],
                       w_gate.shape[2], 8, _tile(d, LANE))
    gb, db_, mb, vb_ = [[in_l[j][0], rows_l[j][0], rows_l[j][1], rows_l[j][2], gu_l[j][0], gu_l[j][1], rows_l[j][3]]
                        for j in range(4)]

    gm, dm_, mm_, vm_ = adam_packed("adam_mod", g_wmod[None], [w_mod], [m_w_mod], [v_w_mod], nmod, 8, _tile(d, LANE))

    sm_w = [b_mod, norm1_w, q_norm_w, k_norm_w, lanes(a_log), lanes(dt_bias), o_norm_w, norm2_w, conv_w[0]]
    sm_m = [m_b_mod, m_norm1_w, m_q_norm_w, m_k_norm_w, lanes(m_a_log), lanes(m_dt_bias), m_o_norm_w, m_norm2_w, m_conv_w[0]]
    sm_v = [v_b_mod, v_norm1_w, v_q_norm_w, v_k_norm_w, lanes(v_a_log), lanes(v_dt_bias), v_o_norm_w, v_norm2_w, v_conv_w[0]]
    sm_g = [g_bmod, g_n1, g_qn, g_kn, g_al, g_dt, g_on, g_n2, g_conv_me]
    g_pack = _pack(sm_g, LANE, 8)
    gs_, ds_, ms_, vs_ = adam_packed("adam_small", g_pack[None], sm_w, sm_m, sm_v, LANE, 8, g_pack.shape[0])

    def assemble(big_l, mod_l, small_l):
        s_bmod, s_n1, s_qn, s_kn, s_al, s_dt, s_on, s_n2, s_conv = small_l
        b_in, b_pa, b_pb, b_out, b_gate, b_up, b_down = big_l
        return [mod_l[0], s_bmod, s_n1, b_in, s_qn, s_kn, s_conv[None], s_al[:, :heads], s_dt[:, :heads], s_on,
                b_pa, b_pb, b_out, s_n2, b_gate, b_up, b_down]

    outs = [loss, dx[None]]
    for big_l, mod_l, small_l in ((gb, gm, gs_), (db_, dm_, ds_), (mb, mm_, ms_), (vb_, vm_, vs_)):
        outs += assemble(big_l, mod_l, small_l)
    return tuple(outs)
```

```python
import jax
import jax.numpy as jnp
from jax import lax
from jax.experimental import pallas as pl
from jax.experimental.pallas import tpu as pltpu

F32, BF16 = jnp.float32, jnp.bfloat16
EPS = 1e-6
HEAD_DIM = 128
GDN_CHUNK = 64
GDN_GROUP = 256
CONV_TAPS = 4
N_DEV = 8
MESH_AXES = ("x", "y", "c")
ADAM_LR, ADAM_B1, ADAM_B2, ADAM_EPS, ADAM_WD, ADAM_STEP = 0.001, 0.9, 0.999, 1e-08, 0.01, 10
VMEM_LIMIT_BYTES = 56 * 1024 * 1024
LANE = 128
MESH_ID = pl.DeviceIdType.MESH


def _call(body, **kw):
    return pl.pallas_call(body, **kw)


def _params(*sem):
    return pltpu.CompilerParams(dimension_semantics=sem or None, vmem_limit_bytes=VMEM_LIMIT_BYTES)


def _tile(n, target):
    t = (min(n, target) // LANE) * LANE
    while t >= LANE:
        if n % t == 0:
            return t
        t -= LANE
    return n


def _silu(x):
    return x * jax.nn.sigmoid(x)


def _my_id():
    return 4 * lax.axis_index("x") + 2 * lax.axis_index("y") + lax.axis_index("c")


def _exchange(name, src, scatter):
    blk = src.shape[1:] if scatter else src.shape

    def body(src_ref, out_ref, send_sems, recv_sems, local_sem):
        x, y, c = lax.axis_index("x"), lax.axis_index("y"), lax.axis_index("c")
        me = 4 * x + 2 * y + c

        def peer(k):
            kx, ky, kc = (k >> 2) & 1, (k >> 1) & 1, k & 1
            px, py, pc = x ^ kx, y ^ ky, c ^ kc
            return (px, py, pc), 4 * px + 2 * py + pc

        def copy(k):
            dev, pid = peer(k)
            return pltpu.make_async_remote_copy(
                src_ref=src_ref.at[pid] if scatter else src_ref,
                dst_ref=out_ref.at[me],
                send_sem=send_sems.at[k - 1],
                recv_sem=recv_sems.at[k - 1],
                device_id=dev,
                device_id_type=MESH_ID,
            )

        def arrival(k):
            dev, pid = peer(k)
            return pltpu.make_async_remote_copy(
                src_ref=src_ref.at[pid] if scatter else src_ref,
                dst_ref=out_ref.at[pid],
                send_sem=send_sems.at[k - 1],
                recv_sem=recv_sems.at[k - 1],
                device_id=dev,
                device_id_type=MESH_ID,
            )

        mine = pltpu.make_async_copy(src_ref.at[me] if scatter else src_ref, out_ref.at[me], local_sem)
        mine.start()
        for k in range(1, N_DEV):
            copy(k).start()
        for k in range(1, N_DEV):
            arrival(k).wait_recv()
        for k in range(1, N_DEV):
            copy(k).wait_send()
        mine.wait()

    return _call(
        body,
        name=name,
        out_shape=jax.ShapeDtypeStruct((N_DEV,) + tuple(blk), src.dtype),
        in_specs=[pl.BlockSpec(memory_space=pl.ANY)],
        out_specs=pl.BlockSpec(memory_space=pl.ANY),
        scratch_shapes=[
            pltpu.SemaphoreType.DMA((N_DEV - 1,)),
            pltpu.SemaphoreType.DMA((N_DEV - 1,)),
            pltpu.SemaphoreType.DMA,
        ],
    )(src)


def _gather_two_level(name, src):
    def body(src_ref, out_ref, send_sems, recv_sems, local_sem):
        x, y, c = lax.axis_index("x"), lax.axis_index("y"), lax.axis_index("c")
        me, sibling = (x, y, c), (x, y, 1 - c)
        chips = [(1 - x, y), (x, 1 - y), (1 - x, 1 - y)]

        def slot(px, py, pc):
            return out_ref.at[4 * px + 2 * py + pc]

        def copy(k, block, to, from_src=False):
            return pltpu.make_async_remote_copy(
                src_ref=src_ref if from_src else slot(*block),
                dst_ref=slot(*block),
                send_sem=send_sems.at[k],
                recv_sem=recv_sems.at[k],
                device_id=to,
                device_id_type=MESH_ID,
            )

        mine = pltpu.make_async_copy(src_ref, slot(*me), local_sem)
        mine.start()
        first = [copy(0, me, sibling, True)] + [copy(1 + j, me, (*chip, c), True) for j, chip in enumerate(chips)]
        for cp in first:
            cp.start()
        passed = [copy(4 + j, (*chip, c), sibling) for j, chip in enumerate(chips)]
        for j, chip in enumerate(chips):
            copy(1 + j, (*chip, c), me).wait_recv()
            passed[j].start()
        copy(0, sibling, me).wait_recv()
        for j, chip in enumerate(chips):
            copy(4 + j, (*chip, 1 - c), me).wait_recv()
        for cp in first + passed:
            cp.wait_send()
        mine.wait()

    return _call(
        body,
        name=name,
        out_shape=jax.ShapeDtypeStruct((N_DEV,) + tuple(src.shape), src.dtype),
        in_specs=[pl.BlockSpec(memory_space=pl.ANY)],
        out_specs=pl.BlockSpec(memory_space=pl.ANY),
        scratch_shapes=[
            pltpu.SemaphoreType.DMA((N_DEV - 1,)),
            pltpu.SemaphoreType.DMA((N_DEV - 1,)),
            pltpu.SemaphoreType.DMA,
        ],
    )(src)


def _mm(name, a, b, *, ta=False, tb=False, out_dtype=F32, a_fn=None, tm=512, tn=512, tk=512):
    m, kdim = (a.shape[1], a.shape[0]) if ta else a.shape
    n = b.shape[0] if tb else b.shape[1]
    assert kdim == (b.shape[1] if tb else b.shape[0]), (a.shape, b.shape, ta, tb)
    tm, tn, tk = _tile(m, tm), _tile(n, tn), _tile(kdim, tk)
    nk = kdim // tk
    a_spec = pl.BlockSpec((tk, tm), lambda i, j, k: (k, i)) if ta else pl.BlockSpec((tm, tk), lambda i, j, k: (i, k))
    b_spec = pl.BlockSpec((tn, tk), lambda i, j, k: (j, k)) if tb else pl.BlockSpec((tk, tn), lambda i, j, k: (k, j))
    dims = (((0 if ta else 1,), (1 if tb else 0,)), ((), ()))

    def body(a_ref, b_ref, o_ref, acc_ref):
        k = pl.program_id(2)
        av = a_ref[...]
        if a_fn is not None:
            av = a_fn(av.astype(F32))
        p = lax.dot_general(av.astype(BF16), b_ref[...].astype(BF16), dims, preferred_element_type=F32)
        if nk == 1:
            o_ref[...] = p.astype(out_dtype)
        else:
            @pl.when(k == 0)
            def _():
                acc_ref[...] = p

            @pl.when(k > 0)
            def _():
                acc_ref[...] += p

            @pl.when(k == nk - 1)
            def _():
                o_ref[...] = acc_ref[...].astype(out_dtype)

    return _call(
        body,
        name=name,
        grid=(m // tm, n // tn, nk),
        in_specs=[a_spec, b_spec],
        out_specs=pl.BlockSpec((tm, tn), lambda i, j, k: (i, j)),
        out_shape=jax.ShapeDtypeStruct((m, n), out_dtype),
        scratch_shapes=[pltpu.VMEM((tm, tn) if nk > 1 else (8, LANE), F32)],
        compiler_params=_params("parallel", "parallel", "arbitrary"),
    )(a, b)


def _row(arr, w=None, off=0, stride=0):
    return (arr, "row", arr.shape[1] if w is None else w, off, stride)


def _par(arr, w=None, off=0, stride=0):
    return (arr, "par", arr.shape[1] if w is None else w, off, stride)


def _seg(name, fn, ins, outs, rows, tm, nh=1):
    nrow = rows // tm

    def spec(kind, w, off, stride):
        if kind == "row":
            return pl.BlockSpec((tm, w), lambda h, i: (i, off + stride * h))
        return pl.BlockSpec((1, w), lambda h, i: (0, off + stride * h))

    in_specs = [spec(kind, w, off, stride) for (_, kind, w, off, stride) in ins]
    out_specs = [spec("row" if kind == "row" else "par", w, off, stride) for (_, _, kind, w, off, stride) in outs]
    out_shape = [
        jax.ShapeDtypeStruct((rows if kind == "row" else 1, ncols), dt) for (ncols, dt, kind, _, _, _) in outs
    ]
    n_in = len(ins)

    def body(*refs):
        h, i = pl.program_id(0), pl.program_id(1)
        vals = fn(*[r[...] for r in refs[:n_in]])
        for (_, dt, kind, _, _, stride), ref, val in zip(outs, refs[n_in:], vals):
            if kind == "row":
                ref[...] = val.astype(dt)
            else:
                first = (i == 0) if stride != 0 else jnp.logical_and(i == 0, h == 0)

                @pl.when(first)
                def _(ref=ref, val=val):
                    ref[...] = val.astype(F32)

                @pl.when(jnp.logical_not(first))
                def _(ref=ref, val=val):
                    ref[...] += val.astype(F32)

    res = _call(
        body,
        name=name,
        grid=(nh, nrow),
        in_specs=in_specs,
        out_specs=out_specs,
        out_shape=out_shape,
        compiler_params=_params("arbitrary", "arbitrary"),
    )(*[t[0] for t in ins])
    return res


def _rmsn(x, w):
    return x * lax.rsqrt(jnp.mean(x * x, axis=-1, keepdims=True) + EPS) * w


def _f_mod(x, nw, sc, sh):
    return _rmsn(x, nw) * (1.0 + sc) + sh


def _f_res(x, t, g, nw, sc, sh):
    h = x + g * t
    return h, _f_mod(h, nw, sc, sh)


def _f_ff(gg, uu):
    return _silu(gg) * uu


def _f_merge(ya, yb, ga, gb):
    return jax.nn.sigmoid(ga) * ya + jax.nn.sigmoid(gb) * yb


def _f_qk(q, k, qw, kw):
    return _rmsn(q, qw), _rmsn(k, kw)


def _f_post(ob, z, ow):
    return _rmsn(ob, ow) * _silu(z)


def _f32(*xs):
    return [x.astype(F32) for x in xs]


def _log_sigmoid(z):
    return jnp.minimum(z, 0.0) - jnp.log(1.0 + jnp.exp(-jnp.abs(z)))


def _dot2(x, tri):
    hi = x.astype(BF16)
    lo = (x - hi.astype(F32)).astype(BF16)
    return jnp.dot(hi, tri, preferred_element_type=F32) + jnp.dot(lo, tri, preferred_element_type=F32)


def _tri(strict):
    j = lax.broadcasted_iota(jnp.int32, (LANE, LANE), 0)
    s = lax.broadcasted_iota(jnp.int32, (LANE, LANE), 1)
    return ((j > s) if strict else (j >= s)).astype(BF16)


SB_SLAB = 512
SB_UNROLL = 4


def _sb_slabs(bq, slab):
    return [slice(r * slab, (r + 1) * slab) for r in range(bq // slab)]


def _sb_diagonal(jj, n_slabs, slab):
    out = []
    for r in range(n_slabs):
        if jj * LANE >= (r + 1) * slab:
            continue
        out.append((r, "full" if (jj + 1) * LANE <= r * slab else "mask"))
    return out


def _sb_mask(r, jj, slab):
    t = r * slab + lax.broadcasted_iota(jnp.int32, (slab, LANE), 0)
    s = jj * LANE + lax.broadcasted_iota(jnp.int32, (slab, LANE), 1)
    return s < t


_NT = (((1,), (1,)), ((), ()))
_TN = (((0,), (0,)), ((), ()))


def _sb_fwd(qn, kn, vb, heads, bq):
    s_len = qn.shape[0]
    scale = HEAD_DIM ** -0.5
    sub = bq // LANE
    slab = min(SB_SLAB, bq)

    def body(q_ref, k_ref, v_ref, o_ref):
        i = pl.program_id(1)
        tri = _tri(True)
        slabs = _sb_slabs(bq, slab)
        qs = [q_ref[rw, :] for rw in slabs]

        def tile(q, k, v, cl, acc, mask):
            z = lax.dot_general(q, k, _NT, preferred_element_type=F32) * scale
            ls = _log_sigmoid(z)
            lm = ls - z
            if mask is not None:
                lm = jnp.where(mask, lm, 0.0)
            w = jnp.exp(ls + (_dot2(lm, tri) + cl))
            if mask is not None:
                w = jnp.where(mask, w, 0.0)
            return cl + jnp.sum(lm, axis=1, keepdims=True), acc + _dot2(w, v)

        state = [(jnp.zeros((slab, 1), F32), jnp.zeros((slab, LANE), F32)) for _ in slabs]
        for jj in reversed(range(sub)):
            j0 = pl.multiple_of(i * bq + jj * LANE, LANE)
            k, v = k_ref[pl.ds(j0, LANE), :], v_ref[pl.ds(j0, LANE), :]
            for r, kind in _sb_diagonal(jj, len(slabs), slab):
                state[r] = tile(qs[r], k, v, *state[r], _sb_mask(r, jj, slab) if kind == "mask" else None)

        def step(t, carry):
            for u in range(SB_UNROLL):
                j0 = pl.multiple_of((i * sub - 1 - (t * SB_UNROLL + u)) * LANE, LANE)
                k, v = k_ref[pl.ds(j0, LANE), :], v_ref[pl.ds(j0, LANE), :]
                carry = tuple(tile(qs[r], k, v, *carry[r], None) for r in range(len(slabs)))
            return carry

        state = lax.fori_loop(0, (i * sub) // SB_UNROLL, step, tuple(state))
        for r, rw in enumerate(slabs):
            o_ref[rw, :] = state[r][1]

    return _call(
        body,
        name="sb_fwd",
        grid=(heads, s_len // bq),
        in_specs=[
            pl.BlockSpec((bq, LANE), lambda h, i: (i, h)),
            pl.BlockSpec((s_len, LANE), lambda h, i: (0, h)),
            pl.BlockSpec((s_len, LANE), lambda h, i: (0, h)),
        ],
        out_specs=pl.BlockSpec((bq, LANE), lambda h, i: (i, h)),
        out_shape=jax.ShapeDtypeStruct(qn.shape, F32),
        compiler_params=_params("parallel", "arbitrary"),
    )(qn, kn, vb)


def _sb_bwd(qn, kn, vb, o, do, heads, bq):
    s_len = qn.shape[0]
    scale = HEAD_DIM ** -0.5
    sub = bq // LANE
    slab = min(SB_SLAB, bq)

    def body(q_ref, k_ref, v_ref, o_ref, do_ref, dq_ref, dk_ref, dv_ref):
        i = pl.program_id(1)

        @pl.when(i == 0)
        def _():
            dk_ref[...] = jnp.zeros_like(dk_ref)
            dv_ref[...] = jnp.zeros_like(dv_ref)

        tri, tri_inc = _tri(True), _tri(False)
        slabs = _sb_slabs(bq, slab)
        qs = [q_ref[rw, :] for rw in slabs]
        dobs = [do_ref[rw, :].astype(BF16) for rw in slabs]
        etots = [jnp.sum(dobs[r].astype(F32) * o_ref[rw, :], axis=1, keepdims=True) for r, rw in enumerate(slabs)]

        def tile(r, k, v, cl, ce, dq, mask):
            z = lax.dot_general(qs[r], k, _NT, preferred_element_type=F32) * scale
            ls = _log_sigmoid(z)
            lm = ls - z
            if mask is not None:
                lm = jnp.where(mask, lm, 0.0)
            a = jnp.exp(ls + (_dot2(lm, tri) + cl))
            if mask is not None:
                a = jnp.where(mask, a, 0.0)
            e = a * lax.dot_general(dobs[r], v, _NT, preferred_element_type=F32)
            before = etots[r] - (_dot2(e, tri_inc) + ce)
            sig = jnp.exp(ls)
            dz = (e * (1.0 - sig) - sig * before) * scale
            if mask is not None:
                dz = jnp.where(mask, dz, 0.0)
            dzb = dz.astype(BF16)
            dq = dq + jnp.dot(dzb, k, preferred_element_type=F32)
            carry = (cl + jnp.sum(lm, axis=1, keepdims=True), ce + jnp.sum(e, axis=1, keepdims=True), dq)
            return carry, dzb, a.astype(BF16)

        def scatter_kv(j0, dzbs, abs_, which):
            cat = lambda xs: xs[0] if len(xs) == 1 else jnp.concatenate(xs, axis=0)
            qcat, docat = cat([qs[r] for r in which]), cat([dobs[r] for r in which])
            dk_ref[pl.ds(j0, LANE), :] += lax.dot_general(cat(dzbs), qcat, _TN, preferred_element_type=F32)
            dv_ref[pl.ds(j0, LANE), :] += lax.dot_general(cat(abs_), docat, _TN, preferred_element_type=F32)

        zero = jnp.zeros((slab, 1), F32)
        state = [(zero, zero, jnp.zeros((slab, LANE), F32)) for _ in slabs]
        for jj in reversed(range(sub)):
            j0 = pl.multiple_of(i * bq + jj * LANE, LANE)
            k, v = k_ref[pl.ds(j0, LANE), :], v_ref[pl.ds(j0, LANE), :]
            dzbs, abs_, which = [], [], []
            for r, kind in _sb_diagonal(jj, len(slabs), slab):
                state[r], dzb, ab = tile(r, k, v, *state[r], _sb_mask(r, jj, slab) if kind == "mask" else None)
                dzbs.append(dzb)
                abs_.append(ab)
                which.append(r)
            scatter_kv(j0, dzbs, abs_, which)

        def step(t, carry):
            for u in range(SB_UNROLL):
                j0 = pl.multiple_of((i * sub - 1 - (t * SB_UNROLL + u)) * LANE, LANE)
                k, v = k_ref[pl.ds(j0, LANE), :], v_ref[pl.ds(j0, LANE), :]
                outs = [tile(r, k, v, *carry[r], None) for r in range(len(slabs))]
                scatter_kv(j0, [o[1] for o in outs], [o[2] for o in outs], list(range(len(slabs))))
                carry = tuple(o[0] for o in outs)
            return carry

        state = lax.fori_loop(0, (i * sub) // SB_UNROLL, step, tuple(state))
        for r, rw in enumerate(slabs):
            dq_ref[rw, :] = state[r][2]

    tile_spec = pl.BlockSpec((bq, LANE), lambda h, i: (i, h))
    full = pl.BlockSpec((s_len, LANE), lambda h, i: (0, h))
    shp = jax.ShapeDtypeStruct(qn.shape, F32)
    return _call(
        body,
        name="sb_bwd",
        grid=(heads, s_len // bq),
        in_specs=[tile_spec, full, full, tile_spec, tile_spec],
        out_specs=[tile_spec, full, full],
        out_shape=[shp, shp, shp],
        compiler_params=_params("parallel", "arbitrary"),
    )(qn, kn, vb, o, do)


def _shift_down(cur, halo, k):
    if k == 0:
        return cur
    r = pltpu.roll(cur, k, 0)
    p = pltpu.roll(halo, k, 0)
    top = jnp.where(lax.broadcasted_iota(jnp.int32, halo.shape, 0) < k, p, r[:8])
    return jnp.concatenate([top, r[8:]], axis=0)


def _shift_up(cur, halo, k):
    if k == 0:
        return cur
    n = cur.shape[0]
    r = pltpu.roll(cur, n - k, 0)
    p = pltpu.roll(halo, 8 - k, 0)
    bot = jnp.where(lax.broadcasted_iota(jnp.int32, halo.shape, 0) >= 8 - k, p, r[n - 8:])
    return jnp.concatenate([r[: n - 8], bot], axis=0)


def _f_qkv_act(pre, jb, heads):
    act = _silu(pre)
    nrm = act * lax.rsqrt(jnp.sum(act * act, axis=-1, keepdims=True) + EPS)
    nrm = nrm * jnp.where(jb < heads, HEAD_DIM ** -0.5, 1.0)
    return jnp.where(jb < 2 * heads, nrm, act)


def _taps(w_ref):
    return [w_ref[j:j + 1, :] for j in range(CONV_TAPS)]


def _conv_pre(x_ref, w, r0, tr):
    cur = x_ref[pl.ds(r0, tr), :]
    halo = x_ref[pl.ds(jnp.maximum(r0 - 8, 0), 8), :]
    halo = jnp.where(r0 > 0, halo, 0.0)
    shifted = [_shift_down(cur, halo, CONV_TAPS - 1 - j) for j in range(CONV_TAPS)]
    pre = sum(w[j] * shifted[j] for j in range(CONV_TAPS))
    return pre, shifted


def _bprep_fwd(proj, col0, conv_w8, heads, tr):
    s_len = proj.shape[0]
    ncol = 3 * heads
    cb0 = col0 // LANE

    def body(x_ref, w_ref, o_ref):
        jb = pl.program_id(0)
        w = _taps(w_ref)

        def step(t, _):
            r0 = pl.multiple_of(t * tr, tr)
            pre, _unused = _conv_pre(x_ref, w, r0, tr)
            o_ref[pl.ds(r0, tr), :] = _f_qkv_act(pre, jb, heads)
            return 0

        lax.fori_loop(0, s_len // tr, step, 0)

    return _call(
        body,
        name="bprep_fwd",
        grid=(ncol,),
        in_specs=[
            pl.BlockSpec((s_len, LANE), lambda j: (0, cb0 + j)),
            pl.BlockSpec((8, LANE), lambda j: (0, j)),
        ],
        out_specs=pl.BlockSpec((s_len, LANE), lambda j: (0, j)),
        out_shape=jax.ShapeDtypeStruct((s_len, ncol * LANE), F32),
        compiler_params=_params("parallel"),
    )(proj, conv_w8)


def _bprep_bwd(proj, col0, conv_w8, dqkv, heads, tr):
    s_len = proj.shape[0]
    ncol = 3 * heads
    cb0 = col0 // LANE
    nt = s_len // tr

    def body(x_ref, w_ref, d_ref, dx_ref, dw_ref, dpre_ref):
        jb = pl.program_id(0)
        w = _taps(w_ref)

        def pass1(t, dws):
            r0 = pl.multiple_of(t * tr, tr)
            pre, shifted = _conv_pre(x_ref, w, r0, tr)
            _, vjp = jax.vjp(lambda p: _f_qkv_act(p, jb, heads), pre)
            (dpre,) = vjp(d_ref[pl.ds(r0, tr), :])
            dpre_ref[pl.ds(r0, tr), :] = dpre
            return tuple(dws[j] + jnp.sum(dpre * shifted[j], axis=0, keepdims=True) for j in range(CONV_TAPS))

        dws = lax.fori_loop(0, nt, pass1, tuple(jnp.zeros((1, LANE), F32) for _ in range(CONV_TAPS)))
        for j in range(CONV_TAPS):
            dw_ref[j:j + 1, :] = dws[j]
        dw_ref[CONV_TAPS:, :] = jnp.zeros((8 - CONV_TAPS, LANE), F32)

        def pass2(t, _):
            r0 = pl.multiple_of(t * tr, tr)
            cur = dpre_ref[pl.ds(r0, tr), :]
            halo = dpre_ref[pl.ds(jnp.minimum(r0 + tr, s_len - 8), 8), :]
            halo = jnp.where(r0 + tr < s_len, halo, 0.0)
            dx = sum(w[j] * _shift_up(cur, halo, CONV_TAPS - 1 - j) for j in range(CONV_TAPS))
            dx_ref[pl.ds(r0, tr), :] = dx.astype(BF16)
            return 0

        lax.fori_loop(0, nt, pass2, 0)

    col = pl.BlockSpec((s_len, LANE), lambda j: (0, j))
    w_spec = pl.BlockSpec((8, LANE), lambda j: (0, j))
    return _call(
        body,
        name="bprep_bwd",
        grid=(ncol,),
        in_specs=[pl.BlockSpec((s_len, LANE), lambda j: (0, cb0 + j)), w_spec, col],
        out_specs=[col, w_spec],
        out_shape=[jax.ShapeDtypeStruct((s_len, ncol * LANE), BF16), jax.ShapeDtypeStruct((8, ncol * LANE), F32)],
        scratch_shapes=[pltpu.VMEM((s_len, LANE), F32)],
        compiler_params=_params("parallel"),
    )(proj, conv_w8, dqkv)


def _f_gbeta(bt, at, a_log, dt_bias):
    xs = at + dt_bias
    softplus = jnp.maximum(xs, 0.0) + jnp.log(1.0 + jnp.exp(-jnp.abs(xs)))
    return jax.nn.sigmoid(bt), -jnp.exp(a_log) * softplus


def _small(name, fn, ins, out_shapes):
    def body(*refs):
        vals = fn(*[r[...] for r in refs[:len(ins)]])
        for ref, val in zip(refs[len(ins):], vals):
            ref[...] = val.astype(ref.dtype)

    return _call(body, name=name, out_shape=[jax.ShapeDtypeStruct(s, F32) for s in out_shapes],
                 compiler_params=_params())(*ins)


def _dot3(a, b, dims=(((1,), (0,)), ((), ()))):
    ah, bh = a.astype(BF16), b.astype(BF16)
    al, bl = (a - ah.astype(F32)).astype(BF16), (b - bh.astype(F32)).astype(BF16)
    d = lambda u, v: lax.dot_general(u, v, dims, preferred_element_type=F32)
    return d(ah, bh) + (d(ah, bl) + d(al, bh))


def _bdot(a, b, dims=(((1,), (0,)), ((), ()))):
    return lax.dot_general(a.astype(BF16), b.astype(BF16), dims, preferred_element_type=F32)


@jax.custom_vjp
def _unit_lower_inverse(nmat):
    r = lax.broadcasted_iota(jnp.int32, nmat.shape, 0)
    i = lax.broadcasted_iota(jnp.int32, nmat.shape, 1)
    am = -nmat
    tinv = jnp.where(r == i, 1.0, 0.0) + am
    for _ in range(GDN_CHUNK.bit_length() - 2):
        am = _dot3(am, am)
        tinv = tinv + _dot3(tinv, am)
    return tinv


def _unit_lower_inverse_fwd(nmat):
    tinv = _unit_lower_inverse(nmat)
    return tinv, tinv


def _unit_lower_inverse_bwd(tinv, dt):
    return (-_dot3(_dot3(tinv, dt, _TN), tinv, _NT),)


_unit_lower_inverse.defvjp(_unit_lower_inverse_fwd, _unit_lower_inverse_bwd)


def _gdn_group(s0, q, k, v, g_row, b_row):
    gs = q.shape[0]
    r = lax.broadcasted_iota(jnp.int32, (gs, gs), 0)
    i = lax.broadcasted_iota(jnp.int32, (gs, gs), 1)
    same = (r // GDN_CHUNK) == (i // GDN_CHUNK)
    g_b = jnp.broadcast_to(g_row, (gs, gs))
    b_b = jnp.broadcast_to(b_row, (gs, gs))
    eye = r == i
    g_col = jnp.sum(jnp.where(eye, g_b, 0.0), axis=1, keepdims=True)
    b_col = jnp.sum(jnp.where(eye, b_b, 0.0), axis=1, keepdims=True)
    gc_col = jnp.sum(jnp.where(same & (i <= r), g_b, 0.0), axis=1, keepdims=True)
    gc_row = jnp.sum(jnp.where(same & (r <= i), jnp.broadcast_to(g_col, (gs, gs)), 0.0), axis=0, keepdims=True)
    gl_col = jnp.sum(jnp.where(same, g_b, 0.0), axis=1, keepdims=True)
    tril = same & (i <= r)
    decay = jnp.where(tril, jnp.exp(jnp.where(tril, gc_col - gc_row, 0.0)), 0.0)
    kk = _bdot(k, k, _NT)
    nmat = jnp.where(same & (i < r), b_col * kk * decay, 0.0)
    tinv = _unit_lower_inverse(nmat)
    eg = jnp.exp(gc_col)
    w_v = _dot3(tinv, b_col * v)
    w_k = _dot3(tinv, (b_col * eg) * k)
    attn = _bdot(q, k, _NT) * decay
    q_g = q * eg
    k_dec = k * jnp.exp(gl_col - gc_col)
    ridx = lax.broadcasted_iota(jnp.int32, (gs, 1), 0)
    s = s0
    u_all = jnp.zeros_like(v)
    o_inter = jnp.zeros_like(v)
    for c in range(gs // GDN_CHUNK):
        in_chunk = (ridx // GDN_CHUNK) == c
        u = jnp.where(in_chunk, w_v - _bdot(w_k, s), 0.0)
        o_inter = o_inter + jnp.where(in_chunk, _bdot(q_g, s), 0.0)
        u_all = u_all + u
        gl = jnp.sum(jnp.where(ridx == c * GDN_CHUNK, gl_col, 0.0), axis=0, keepdims=True)
        s = jnp.exp(gl) * s + _bdot(k_dec, u, _TN)
    return o_inter + _bdot(attn, u_all), s


GDN_PAIR = 2


def _gdn_specs(heads, ngrp, rev):
    blk = (lambda n: ngrp - 1 - n) if rev else (lambda n: n)
    width = GDN_PAIR * LANE
    def col(off):
        return pl.BlockSpec((GDN_GROUP, width), lambda h, n: (blk(n), (off * heads) // GDN_PAIR + h))
    vec = pl.BlockSpec((GDN_PAIR, ngrp, GDN_GROUP), lambda h, n: (h, 0, 0))
    state = pl.BlockSpec((GDN_PAIR, 1, HEAD_DIM, HEAD_DIM), lambda h, n: (h, blk(n), 0, 0))
    out_col = pl.BlockSpec((GDN_GROUP, width), lambda h, n: (blk(n), h))
    return col, vec, state, out_col


def _gdn_fwd(qkv, g3, b3, heads):
    s_len = qkv.shape[0]
    ngrp = s_len // GDN_GROUP
    col, vec, state, out_col = _gdn_specs(heads, ngrp, False)

    def body(q_ref, k_ref, v_ref, g_ref, b_ref, o_ref, st_ref, s_scr):
        n = pl.program_id(1)

        @pl.when(n == 0)
        def _():
            s_scr[...] = jnp.zeros_like(s_scr)

        for a in range(GDN_PAIR):
            ln = slice(a * LANE, (a + 1) * LANE)
            s0 = s_scr[a]
            st_ref[a, 0] = s0
            o, s1 = _gdn_group(s0, q_ref[:, ln], k_ref[:, ln], v_ref[:, ln],
                               g_ref[a, pl.ds(n, 1), :], b_ref[a, pl.ds(n, 1), :])
            o_ref[:, ln] = o
            s_scr[a] = s1

    return _call(
        body,
        name="gdn_fwd",
        grid=(heads // GDN_PAIR, ngrp),
        in_specs=[col(0), col(1), col(2), vec, vec],
        out_specs=[out_col, state],
        out_shape=[
            jax.ShapeDtypeStruct((s_len, heads * LANE), F32),
            jax.ShapeDtypeStruct((heads, ngrp, HEAD_DIM, HEAD_DIM), F32),
        ],
        scratch_shapes=[pltpu.VMEM((GDN_PAIR, HEAD_DIM, HEAD_DIM), F32)],
        compiler_params=_params("parallel", "arbitrary"),
    )(qkv, qkv, qkv, g3, b3)


def _gdn_bwd(qkv, g3, b3, states, do, heads):
    s_len = qkv.shape[0]
    ngrp = s_len // GDN_GROUP
    col, vec, state, out_col = _gdn_specs(heads, ngrp, True)

    def body(q_ref, k_ref, v_ref, g_ref, b_ref, st_ref, do_ref, dq_ref, dk_ref, dv_ref, dg_ref, db_ref, ds_scr):
        n = pl.program_id(1)
        grp = ngrp - 1 - n

        @pl.when(n == 0)
        def _():
            ds_scr[...] = jnp.zeros_like(ds_scr)

        for a in range(GDN_PAIR):
            ln = slice(a * LANE, (a + 1) * LANE)
            _, vjp = jax.vjp(_gdn_group, st_ref[a, 0], q_ref[:, ln], k_ref[:, ln], v_ref[:, ln],
                             g_ref[a, pl.ds(grp, 1), :], b_ref[a, pl.ds(grp, 1), :])
            ds0, dq, dk, dv, dg, db = vjp((do_ref[:, ln], ds_scr[a]))
            dq_ref[:, ln] = dq
            dk_ref[:, ln] = dk
            dv_ref[:, ln] = dv
            dg_ref[a, pl.ds(grp, 1), :] = dg
            db_ref[a, pl.ds(grp, 1), :] = db
            ds_scr[a] = ds0

    shp = jax.ShapeDtypeStruct((s_len, heads * LANE), F32)
    vshp = jax.ShapeDtypeStruct(g3.shape, F32)
    return _call(
        body,
        name="gdn_bwd",
        grid=(heads // GDN_PAIR, ngrp),
        in_specs=[col(0), col(1), col(2), vec, vec, state, out_col],
        out_specs=[out_col, out_col, out_col, vec, vec],
        out_shape=[shp, shp, shp, vshp, vshp],
        scratch_shapes=[pltpu.VMEM((GDN_PAIR, HEAD_DIM, HEAD_DIM), F32)],
        compiler_params=_params("parallel", "arbitrary"),
    )(qkv, qkv, qkv, g3, b3, states, do)


def _sum_adam(name, parts, w, m, v, tr):
    n, rows, cols = parts.shape

    def body(p_ref, w_ref, m_ref, v_ref, g_out, d_out, m_out, v_out):
        g = p_ref[0].astype(F32)
        for s in range(1, n):
            g = g + p_ref[s].astype(F32)
        m_new = ADAM_B1 * m_ref[...] + (1.0 - ADAM_B1) * g
        v_new = ADAM_B2 * v_ref[...] + (1.0 - ADAM_B2) * (g * g)
        m_hat = m_new / (1.0 - ADAM_B1 ** ADAM_STEP)
        v_hat = v_new / (1.0 - ADAM_B2 ** ADAM_STEP)
        g_out[...] = g
        d_out[...] = -ADAM_LR * (m_hat / (jnp.sqrt(v_hat) + ADAM_EPS) + ADAM_WD * w_ref[...])
        m_out[...] = m_new
        v_out[...] = v_new

    mat = pl.BlockSpec((tr, cols), lambda i: (i, 0))
    shp = jax.ShapeDtypeStruct((rows, cols), F32)
    return _call(
        body,
        name=name,
        grid=(rows // tr,),
        in_specs=[pl.BlockSpec((n, tr, cols), lambda i: (0, i, 0)), mat, mat, mat],
        out_specs=[mat, mat, mat, mat],
        out_shape=[shp, shp, shp, shp],
        compiler_params=_params("parallel"),
    )(parts, w, m, v)


def _sum_parts(name, parts):
    n = parts.shape[0]

    def fn(p):
        g = p[0]
        for s in range(1, n):
            g = g + p[s]
        return (g,)

    return _small(name, fn, [parts], [parts.shape[1:]])[0]


def _pad_rows(a, mult):
    r = (-a.shape[0]) % mult
    return a if r == 0 else jnp.concatenate([a, jnp.zeros((r,) + a.shape[1:], a.dtype)], axis=0)


def _pad_lanes(a, width):
    return jnp.concatenate([a, jnp.zeros(a.shape[:-1] + (width - a.shape[-1],), a.dtype)], axis=-1)


def _pack(pieces, width, mult):
    return _pad_rows(jnp.concatenate([p.reshape(-1, width) for p in pieces], axis=0), mult)


def _unpack(packed, shapes, width):
    out, r0 = [], 0
    for shp in shapes:
        size = 1
        for d in shp:
            size *= d
        nr = size // width
        out.append(packed[..., r0:r0 + nr, :].reshape(packed.shape[:-2] + tuple(shp)))
        r0 += nr
    return out


def kernel(x, c, w_mod, b_mod, norm1_w, w_in, q_norm_w, k_norm_w, conv_w, a_log, dt_bias, o_norm_w, p_a, p_b, w_out, norm2_w, w_gate, w_up, w_down, loss_target, m_w_mod, m_b_mod, m_norm1_w, m_w_in, m_q_norm_w, m_k_norm_w, m_conv_w, m_a_log, m_dt_bias, m_o_norm_w, m_p_a, m_p_b, m_w_out, m_norm2_w, m_w_gate, m_w_up, m_w_down, v_w_mod, v_b_mod, v_norm1_w, v_w_in, v_q_norm_w, v_k_norm_w, v_conv_w, v_a_log, v_dt_bias, v_o_norm_w, v_p_a, v_p_b, v_w_out, v_norm2_w, v_w_gate, v_w_up, v_w_down):
    s_len, d = x.shape[1], x.shape[2]
    heads = a_log.shape[1]
    dh = heads * HEAD_DIM
    f = w_down.shape[1] * N_DEV
    din_loc = w_in.shape[2]
    me = _my_id()
    x2, tgt = x[0], loss_target[0]

    row_sharded = [p_a, p_b, w_out, w_down]
    wg = _gather_two_level("gather_rows", _pack([t[0].astype(BF16) for t in row_sharded], d, LANE))
    g_pa, g_pb, g_out, g_down = _unpack(wg, [t.shape[1:] for t in row_sharded], d)
    g_in = _gather_two_level("gather_w_in", w_in[0].astype(BF16))
    g_gu = _gather_two_level("gather_w_gu", jnp.stack([w_gate[0], w_up[0]]).astype(BF16))
    g_gate, g_up = g_gu[:, 0], g_gu[:, 1]
    cols = lambda t: t.transpose(1, 0, 2).reshape(t.shape[1], -1)
    rows = lambda t: t.reshape(-1, t.shape[2])
    w_in_g = cols(g_in)
    o_ba = 3 * dh + 3 * dh + dh
    din = w_in_g.shape[1]
    n_perm = o_ba + 2 * d + LANE
    w_in_p = jnp.concatenate(
        [w_in_g[:, :o_ba], w_in_g[:, o_ba + 2 * heads:], w_in_g[:, o_ba:o_ba + 2 * heads],
         jnp.zeros((d, LANE - 2 * heads), BF16)], axis=1)
    o_qkvb, o_z, o_ga, o_bad = 3 * dh, 6 * dh, 7 * dh, 7 * dh + 2 * d
    w_pa, w_pb, w_o, w_dn = rows(g_pa), rows(g_pb), rows(g_out), rows(g_down)
    w_gu = jnp.concatenate([cols(g_gate), cols(g_up)], axis=1)

    c_all = _pad_rows(_exchange("gather_c", c, False).reshape(N_DEV, d), LANE)
    mod_part = _mm("mod_fwd", c_all, w_mod[0], a_fn=_silu, tk=d)[:N_DEV]
    mod_all = _exchange("gather_mod", mod_part, False)
    mod_me = lax.dynamic_index_in_dim(mod_all, me, axis=1, keepdims=False).reshape(1, 6 * d) + b_mod
    sh1, sc1, gt1, sh2, sc2, gt2 = [mod_me[:, j * d:(j + 1) * d] for j in range(6)]

    tm = min(128, s_len)
    th = min(1024, s_len)
    (u1,) = _seg("pre1_fwd", lambda a, nw, sc, sh: (_f_mod(a, nw, sc, sh),),
                 [_row(x2), _par(norm1_w), _par(sc1), _par(sh1)], [(d, BF16, "row", d, 0, 0)], s_len, tm)
    proj = _mm("proj", u1, w_in_p, tm=1024, tn=640, tk=d)

    hb = dh // LANE
    qn, kn, vb = _seg(
        "qk_fwd", lambda q, k, v, qw, kw: _f_qk(q, k, qw, kw) + (v,),
        [_row(proj, LANE, 0, 1), _row(proj, LANE, hb, 1), _row(proj, LANE, 2 * hb, 1), _par(q_norm_w), _par(k_norm_w)],
        [(dh, BF16, "row", LANE, 0, 1)] * 3, s_len, th, nh=hb)
    o_a = _sb_fwd(qn, kn, vb, heads, min(1024, s_len))

    conv_all = _exchange("gather_conv", conv_w[0], False)
    conv_g = _pad_rows(conv_all.transpose(1, 0, 2).reshape(CONV_TAPS, 3 * dh), 8)
    tr = min(512, s_len)
    qkv_b = _bprep_fwd(proj, o_qkvb, conv_g, heads, tr)
    ba_t = proj[:, o_bad:o_bad + 2 * heads].T
    a_col, dt_col = a_log.reshape(heads, 1), dt_bias.reshape(heads, 1)
    beta_t, g_t = _small("gbeta_fwd", _f_gbeta, [ba_t[:heads], ba_t[heads:], a_col, dt_col], [(heads, s_len)] * 2)
    ngrp = s_len // GDN_GROUP
    g3, b3 = g_t.reshape(heads, ngrp, GDN_GROUP), beta_t.reshape(heads, ngrp, GDN_GROUP)
    o_b, states = _gdn_fwd(qkv_b, g3, b3, heads)
    zoff = o_z // LANE
    (ob2,) = _seg("post_fwd", lambda ob, z, ow: (_f_post(ob, z, ow),),
                  [_row(o_b, LANE, 0, 1), _row(proj, LANE, zoff, 1), _par(o_norm_w)],
                  [(dh, BF16, "row", LANE, 0, 1)], s_len, th, nh=hb)

    ya = _mm("ya", o_a, w_pa, tm=1024, tk=dh)
    yb = _mm("yb", ob2, w_pb, tm=1024, tk=dh)
    goff = o_ga // d
    gate_ins = [_row(proj, d, goff, 0), _row(proj, d, goff + 1, 0)]
    (merged,) = _seg("merge_fwd", lambda a, b, ga, gb: (_f_merge(a, b, ga, gb),),
                     [_row(ya), _row(yb)] + gate_ins, [(d, BF16, "row", d, 0, 0)], s_len, tm)
    t_out = _mm("attn_out", merged, w_o, tm=1024, tk=d)
    res_par = [_par(gt1), _par(norm2_w), _par(sc2), _par(sh2)]
    h1, u2 = _seg("res1_fwd", _f_res, [_row(x2), _row(t_out)] + res_par,
                  [(d, F32, "row", d, 0, 0), (d, BF16, "row", d, 0, 0)], s_len, tm)
    gu = _mm("ffn_in", u2, w_gu, tm=1024, tk=d)
    tf = 128
    (ff,) = _seg("ff_fwd", lambda a, b: (_f_ff(a, b),), [_row(gu, f, 0, 0), _row(gu, f, 1, 0)],
                 [(f, BF16, "row", f, 0, 0)], s_len, tf)
    dn = _mm("ffn_out", ff, w_dn, tm=1024, tk=f // 2)

    def f_loss(hh, dd, g2, tg):
        err = hh + g2 * dd - tg
        dh2 = err * (1.0 / d)
        return (dh2, g2 * dh2, jnp.sum(err * err, axis=0, keepdims=True), jnp.sum(dh2 * dd, axis=0, keepdims=True))

    dh2, ddn, lsq, dgt2 = _seg(
        "loss", f_loss, [_row(h1), _row(dn), _par(gt2), _row(tgt)],
        [(d, F32, "row", d, 0, 0), (d, BF16, "row", d, 0, 0), (d, F32, "acc", d, 0, 0), (d, F32, "acc", d, 0, 0)],
        s_len, tm)
    loss = lax.psum(0.5 * jnp.sum(lsq) / d, MESH_AXES)

    dff = _mm("d_ff", ddn, w_dn, tb=True, tm=1024, tk=d)
    dw_down = _mm("dw_down", ff, ddn, ta=True, out_dtype=BF16, tk=2048)

    def f_ff_bwd(a, b, dy):
        _, vjp = jax.vjp(_f_ff, a, b)
        da, db = vjp(dy)
        return (jnp.concatenate([da, db], axis=1),)

    (dgu,) = _seg("ff_bwd", f_ff_bwd, [_row(gu, f, 0, 0), _row(gu, f, 1, 0), _row(dff)],
                  [(2 * f, BF16, "row", 2 * f, 0, 0)], s_len, tf)
    du2 = _mm("d_u2", dgu, w_gu, tb=True, tm=1024, tk=f // 2)
    dw_gu = _mm("dw_gu", u2, dgu, ta=True, out_dtype=BF16, tm=1024, tk=2048)

    def f_res_bwd(a, t, g, nw, sc, sh, dhd, du):
        _, vjp = jax.vjp(_f_res, a, t, g, nw, sc, sh)
        da, dt_, dg, dnw, dsc, dsh = vjp((dhd, du))
        return da, dt_, dg, dnw, dsc, dsh

    acc_d = (d, F32, "acc", d, 0, 0)
    dh1, dt_out, dgt1, dnorm2, dsc2, dsh2 = _seg(
        "res1_bwd", f_res_bwd, [_row(x2), _row(t_out)] + res_par + [_row(dh2), _row(du2)],
        [(d, F32, "row", d, 0, 0), (d, BF16, "row", d, 0, 0), acc_d, acc_d, acc_d, acc_d], s_len, tm)

    dmerged = _mm("d_merged", dt_out, w_o, tb=True, tm=1024, tk=d)
    dw_out = _mm("dw_out", merged, dt_out, ta=True, out_dtype=BF16, tm=1024, tk=2048)

    def f_merge_bwd(a, b, ga, gb, dy):
        _, vjp = jax.vjp(_f_merge, a, b, ga, gb)
        da, db, dga, dgb = vjp(dy)
        return da, db, jnp.concatenate([dga, dgb], axis=1)

    dya, dyb, dgates = _seg("merge_bwd", f_merge_bwd, [_row(ya), _row(yb)] + gate_ins + [_row(dmerged)],
                            [(d, BF16, "row", d, 0, 0), (d, BF16, "row", d, 0, 0), (2 * d, BF16, "row", 2 * d, 0, 0)],
                            s_len, tm)
    do_a = _mm("d_oa", dya, w_pa, tb=True, tm=1024, tk=d)
    dw_pa = _mm("dw_pa", o_a, dya, ta=True, out_dtype=BF16, tk=2048)
    dob2 = _mm("d_ob2", dyb, w_pb, tb=True, tm=1024, tk=d)
    dw_pb = _mm("dw_pb", ob2, dyb, ta=True, out_dtype=BF16, tm=1024, tk=2048)

    def f_post_bwd(ob, z, ow, dy):
        _, vjp = jax.vjp(_f_post, ob, z, ow)
        return vjp(dy)

    acc_h = (LANE, F32, "acc", LANE, 0, 0)
    d_ob, dz, d_onorm = _seg(
        "post_bwd", f_post_bwd,
        [_row(o_b, LANE, 0, 1), _row(proj, LANE, zoff, 1), _par(o_norm_w), _row(dob2, LANE, 0, 1)],
        [(dh, F32, "row", LANE, 0, 1), (dh, BF16, "row", LANE, 0, 1), acc_h], s_len, th, nh=hb)
    dqb, dkb, dvb, dg3, db3 = _gdn_bwd(qkv_b, g3, b3, states, d_ob, heads)
    dqkv_n = jnp.concatenate([dqb, dkb, dvb], axis=1)
    dqkv_pre, dconv8 = _bprep_bwd(proj, o_qkvb, conv_g, dqkv_n, heads, tr)

    def f_gbeta_bwd(bt, at, al, dtb, dbeta, dg):
        _, vjp = jax.vjp(_f_gbeta, bt, at, al, dtb)
        return vjp((dbeta, dg))

    dbt, dat, da_log, ddt = _small(
        "gbeta_bwd", f_gbeta_bwd,
        [ba_t[:heads], ba_t[heads:], a_col, dt_col, db3.reshape(heads, s_len), dg3.reshape(heads, s_len)],
        [(heads, s_len), (heads, s_len), (heads, 1), (heads, 1)])
    dba = _pad_lanes(jnp.concatenate([dbt, dat], axis=0).T, LANE).astype(BF16)

    dqn, dkn, dva = _sb_bwd(qn, kn, vb, o_a, do_a, heads, min(512, s_len))

    def f_qk_bwd(q, k, qw, kw, dq, dk):
        _, vjp = jax.vjp(_f_qk, q, k, qw, kw)
        return vjp((dq, dk))

    dqa, dka, d_qnorm, d_knorm = _seg(
        "qk_bwd", f_qk_bwd,
        [_row(proj, LANE, 0, 1), _row(proj, LANE, hb, 1), _par(q_norm_w), _par(k_norm_w),
         _row(dqn, LANE, 0, 1), _row(dkn, LANE, 0, 1)],
        [(dh, BF16, "row", LANE, 0, 1), (dh, BF16, "row", LANE, 0, 1), acc_h, acc_h], s_len, th, nh=hb)

    dproj = jnp.concatenate([dqa, dka, dva.astype(BF16), dqkv_pre, dz, dgates, dba], axis=1)
    du1 = _mm("d_u1", dproj, w_in_p, tb=True, tm=1024, tk=n_perm // 5)
    dw_in_p = _mm("dw_in", u1, dproj, ta=True, out_dtype=BF16, tm=1024, tn=640, tk=2048)

    def f_pre_bwd(a, nw, sc, sh, du, dres):
        _, vjp = jax.vjp(_f_mod, a, nw, sc, sh)
        da, dnw, dsc, dsh = vjp(du)
        return da + dres, dnw, dsc, dsh

    dx, dnorm1, dsc1, dsh1 = _seg(
        "pre1_bwd", f_pre_bwd, [_row(x2), _par(norm1_w), _par(sc1), _par(sh1), _row(du1), _row(dh1)],
        [(d, F32, "row", d, 0, 0), acc_d, acc_d, acc_d], s_len, tm)

    dmod = jnp.concatenate([dsh1, dsc1, dgt1, dsh2, dsc2, dgt2], axis=1)
    lanes = lambda t: _pad_lanes(t.reshape(1, -1), LANE)
    small = [dmod, dnorm1, d_qnorm, d_knorm, lanes(da_log), lanes(ddt), d_onorm, dnorm2, dconv8[:CONV_TAPS]]
    small_shapes = [t.shape for t in small]
    small_all = _exchange("gather_small", _pack(small, LANE, 8), False)
    small_sum = _sum_parts("sum_small", small_all)
    g_bmod, g_n1, g_qn, g_kn, g_al, g_dt, g_on, g_n2, g_conv = _unpack(small_sum, small_shapes, LANE)
    ncv = conv_w.shape[2]
    g_conv_me = lax.dynamic_slice_in_dim(g_conv, me * ncv, ncv, axis=1)

    nmod = w_mod.shape[2]
    dmod_all = _unpack(small_all, small_shapes[:1], LANE)[0].reshape(N_DEV, 6 * d)
    dmod_cols = _pad_rows(lax.dynamic_slice_in_dim(dmod_all, me * nmod, nmod, axis=1), LANE)
    g_wmod = _mm("dw_mod", c_all, dmod_cols, ta=True, a_fn=_silu, tk=LANE)

    dw_in_g = jnp.concatenate([dw_in_p[:, :o_ba], dw_in_p[:, o_bad:o_bad + 2 * heads], dw_in_p[:, o_ga:o_ga + 2 * d]], axis=1)
    to_cols = lambda t: t.reshape(t.shape[0], N_DEV, -1).transpose(1, 0, 2)
    to_rows = lambda t: t.reshape(N_DEV, -1, t.shape[1])
    send = jnp.concatenate([to_rows(dw_pa), to_rows(dw_pb), to_rows(dw_out), to_rows(dw_down)], axis=1)
    pad = (-send.shape[1]) % LANE
    if pad:
        send = jnp.concatenate([send, jnp.zeros((N_DEV, pad, d), BF16)], axis=1)
    recv_rows = _exchange("scatter_rows", send, True)
    recv_in = _exchange("scatter_w_in", to_cols(dw_in_g), True)
    recv_gu = _exchange("scatter_w_gu", jnp.concatenate([to_cols(dw_gu[:, :f]), to_cols(dw_gu[:, f:])], axis=1), True)

    def adam_packed(name, parts_, ws, ms, vs, width, mult, tr_):
        outs = _sum_adam(name, parts_, _pack(ws, width, mult), _pack(ms, width, mult), _pack(vs, width, mult), tr_)
        shapes = [t.shape for t in ws]
        return [_unpack(o, shapes, width) for o in outs]

    rows_l = adam_packed("adam_rows", recv_rows, row_sharded, [m_p_a, m_p_b, m_w_out, m_w_down],
                         [v_p_a, v_p_b, v_w_out, v_w_down], d, LANE, LANE)
    in_l = adam_packed("adam_w_in", recv_in, [w_in], [m_w_in], [v_w_in], din_loc, 8, _tile(d, LANE) // 2)
    gu_l = adam_packed("adam_w_gu", recv_gu, [w_gate, w_up], [m_w_gate, m_w_up], [v_w_gate, v_w_up],
                       w_gate.shape[2], 8, _tile(d, LANE))
    gb, db_, mb, vb_ = [[in_l[j][0], rows_l[j][0], rows_l[j][1], rows_l[j][2], gu_l[j][0], gu_l[j][1], rows_l[j][3]]
                        for j in range(4)]

    gm, dm_, mm_, vm_ = adam_packed("adam_mod", g_wmod[None], [w_mod], [m_w_mod], [v_w_mod], nmod, 8, _tile(d, LANE))

    sm_w = [b_mod, norm1_w, q_norm_w, k_norm_w, lanes(a_log), lanes(dt_bias), o_norm_w, norm2_w, conv_w[0]]
    sm_m = [m_b_mod, m_norm1_w, m_q_norm_w, m_k_norm_w, lanes(m_a_log), lanes(m_dt_bias), m_o_norm_w, m_norm2_w, m_conv_w[0]]
    sm_v = [v_b_mod, v_norm1_w, v_q_norm_w, v_k_norm_w, lanes(v_a_log), lanes(v_dt_bias), v_o_norm_w, v_norm2_w, v_conv_w[0]]
    sm_g = [g_bmod, g_n1, g_qn, g_kn, g_al, g_dt, g_on, g_n2, g_conv_me]
    g_pack = _pack(sm_g, LANE, 8)
    gs_, ds_, ms_, vs_ = adam_packed("adam_small", g_pack[None], sm_w, sm_m, sm_v, LANE, 8, g_pack.shape[0])

    def assemble(big_l, mod_l, small_l):
        s_bmod, s_n1, s_qn, s_kn, s_al, s_dt, s_on, s_n2, s_conv = small_l
        b_in, b_pa, b_pb, b_out, b_gate, b_up, b_down = big_l
        return [mod_l[0], s_bmod, s_n1, b_in, s_qn, s_kn, s_conv[None], s_al[:, :heads], s_dt[:, :heads], s_on,
                b_pa, b_pb, b_out, s_n2, b_gate, b_up, b_down]

    outs = [loss, dx[None]]
    for big_l, mod_l, small_l in ((gb, gm, gs_), (db_, dm_, ds_), (mb, mm_, ms_), (vb_, vm_, vs_)):
        outs += assemble(big_l, mod_l, small_l)
    return tuple(outs)
```
